```python
import math
import jax, jax.numpy as jnp
from jax import lax
import numpy as np

D_MODEL = 4096
BATCH = 8
SEQ = 2048
DEPTH = 4
DEC_BATCH = 2
DEC_SEQ = 8192
PAST_LEN = 128

N_HEADS = 4
HEAD_DIM = 128
V_DIM = 2 * HEAD_DIM
QK_WIDTH = N_HEADS * 2 * HEAD_DIM
ATTN_WIDTH = N_HEADS * V_DIM
CONV_CH = 1024
CONV_KERNEL = 31
D_FF = 2048
N_META = 16
N_BUCKETS = 32
MAX_DISTANCE = 128
Q_BLOCK = 128
EPS = 1e-6
LN_EPS = 1e-5
D_IN = 2 * QK_WIDTH + ATTN_WIDTH + 2 * CONV_CH + 2 * D_MODEL
SPLITS = (QK_WIDTH, 2 * QK_WIDTH, 2 * QK_WIDTH + ATTN_WIDTH,
          2 * QK_WIDTH + ATTN_WIDTH + 2 * CONV_CH)

kernel_name = "hybrid_gated_diffattn_conformer_encoder"


def _rmsnorm(x, g):
    xf = x.astype(jnp.float32)
    y = xf * lax.rsqrt(jnp.mean(xf * xf, axis=-1, keepdims=True) + EPS)
    return (y * g.astype(jnp.float32)).astype(x.dtype)


def _layernorm(x, g, b):
    xf = x.astype(jnp.float32)
    mu = jnp.mean(xf, axis=-1, keepdims=True)
    var = jnp.mean(jnp.square(xf - mu), axis=-1, keepdims=True)
    y = (xf - mu) * lax.rsqrt(var + LN_EPS)
    return (y * g.astype(jnp.float32) + b.astype(jnp.float32)).astype(x.dtype)


def _swiglu(x, wg, wu, wd):
    return (jax.nn.silu(x @ wg) * (x @ wu)) @ wd


def _t5_bucket(rel):
    nb = N_BUCKETS // 2
    max_exact = nb // 2
    ret = (rel > 0).astype(jnp.int32) * nb
    n = jnp.abs(rel)
    nf = jnp.maximum(n, 1).astype(jnp.float32)
    large = max_exact + (jnp.log(nf / max_exact) / math.log(MAX_DISTANCE / max_exact)
                         * (nb - max_exact)).astype(jnp.int32)
    large = jnp.minimum(large, nb - 1)
    return ret + jnp.where(n < max_exact, n, large)


def _diff_attention(q1, q2, k1, k2, v, lam, rel_bias):
    B, H, L, _ = q1.shape
    nb = -(-L // Q_BLOCK)
    Lp = nb * Q_BLOCK
    pad = ((0, 0), (0, 0), (0, Lp - L), (0, 0))
    qb1 = jnp.pad(q1, pad).reshape(B, H, nb, Q_BLOCK, HEAD_DIM).transpose(2, 0, 1, 3, 4)
    qb2 = jnp.pad(q2, pad).reshape(B, H, nb, Q_BLOCK, HEAD_DIM).transpose(2, 0, 1, 3, 4)
    scale = HEAD_DIM ** -0.5
    kpos = jnp.arange(L, dtype=jnp.int32)

    def block(args):
        a1, a2, bi = args
        qpos = bi * Q_BLOCK + jnp.arange(Q_BLOCK, dtype=jnp.int32)
        bucket = _t5_bucket(kpos[None, :] - qpos[:, None])
        bias = jnp.transpose(rel_bias[bucket], (2, 0, 1)).astype(jnp.float32)
        s1 = jnp.einsum('bhqd,bhkd->bhqk', a1, k1).astype(jnp.float32) * scale + bias
        s2 = jnp.einsum('bhqd,bhkd->bhqk', a2, k2).astype(jnp.float32) * scale + bias
        p = jax.nn.softmax(s1, axis=-1) - lam * jax.nn.softmax(s2, axis=-1)
        return jnp.einsum('bhqk,bhkd->bhqd', p.astype(v.dtype), v)

    out = lax.map(block, (qb1, qb2, jnp.arange(nb, dtype=jnp.int32)))
    out = out.transpose(1, 0, 3, 2, 4).reshape(B, Lp, H, V_DIM)
    return out[:, :L]


def _mixer(xn, layer, w_in, lq1, lk1, lq2, lk2, subln_g, w_attn_out,
           conv_w, conv_b, conv_ln_g, conv_ln_b, w_conv_out, w_out, rel_bias):
    B, L, _ = xn.shape
    proj = xn @ w_in
    q, k, v, u, g = jnp.split(proj, SPLITS, axis=-1)

    q = q.reshape(B, L, N_HEADS, 2, HEAD_DIM).transpose(3, 0, 2, 1, 4)
    k = k.reshape(B, L, N_HEADS, 2, HEAD_DIM).transpose(3, 0, 2, 1, 4)
    v = v.reshape(B, L, N_HEADS, V_DIM).transpose(0, 2, 1, 3)
    lam_init = 0.8 - 0.6 * math.exp(-0.3 * layer)
    lam = (jnp.exp(jnp.sum(lq1.astype(jnp.float32) * lk1.astype(jnp.float32)))
           - jnp.exp(jnp.sum(lq2.astype(jnp.float32) * lk2.astype(jnp.float32)))
           + lam_init)
    att = _diff_attention(q[0], q[1], k[0], k[1], v, lam, rel_bias)
    att = _rmsnorm(att, subln_g) * (1.0 - lam_init)
    att_d = att.reshape(B, L, ATTN_WIDTH) @ w_attn_out

    ua, ub = jnp.split(u, 2, axis=-1)
    c = ua * jax.nn.sigmoid(ub)
    c = lax.conv_general_dilated(
        c, conv_w[:, None, :].astype(c.dtype), window_strides=(1,),
        padding=[(CONV_KERNEL // 2, CONV_KERNEL // 2)],
        dimension_numbers=('NWC', 'WIO', 'NWC'),
        feature_group_count=CONV_CH) + conv_b
    c = jax.nn.silu(_layernorm(c, conv_ln_g, conv_ln_b))
    conv_d = c @ w_conv_out

    g_a, g_c = jnp.split(jax.nn.sigmoid(g), 2, axis=-1)
    return (g_a * att_d + g_c * conv_d) @ w_out


def _encode(x, meta_tokens, rel_bias,
            norm_ffn1, w_ffn1_gate, w_ffn1_up, w_ffn1_down,
            norm_mix, w_in, lambda_q1, lambda_k1, lambda_q2, lambda_k2, subln_g,
            w_attn_out, conv_w, conv_b, conv_ln_g, conv_ln_b, w_conv_out, w_out,
            norm_ffn2, w_ffn2_gate, w_ffn2_up, w_ffn2_down, final_norm):
    B = x.shape[0]
    meta = jnp.broadcast_to(meta_tokens[None].astype(x.dtype), (B, N_META, D_MODEL))
    h = jnp.concatenate([meta, x], axis=1)
    for l in range(DEPTH):
        h = h + 0.5 * _swiglu(_rmsnorm(h, norm_ffn1[l]), w_ffn1_gate[l], w_ffn1_up[l], w_ffn1_down[l])
        h = h + _mixer(_rmsnorm(h, norm_mix[l]), l, w_in[l],
                       lambda_q1[l], lambda_k1[l], lambda_q2[l], lambda_k2[l], subln_g[l],
                       w_attn_out[l], conv_w[l], conv_b[l], conv_ln_g[l], conv_ln_b[l],
                       w_conv_out[l], w_out[l], rel_bias)
        h = h + 0.5 * _swiglu(_rmsnorm(h, norm_ffn2[l]), w_ffn2_gate[l], w_ffn2_up[l], w_ffn2_down[l])
    h = _rmsnorm(h, final_norm)
    return h[:, N_META:]


def setup_inputs(seed: int = 0) -> dict:
    key = jax.random.key(seed)
    ks = jax.random.split(key, 32)
    f32 = jnp.float32

    def w(k, shape, fan_in):
        return jax.random.normal(k, shape, f32) * (fan_in ** -0.5)

    def gain(k, shape):
        return 1.0 + 0.02 * jax.random.normal(k, shape, f32)

    return {
        "x_prompt": jax.random.normal(ks[0], (BATCH, SEQ, D_MODEL), f32),
        "x_sample": jax.random.normal(ks[1], (DEC_BATCH, DEC_SEQ, D_MODEL), f32),
        "meta_tokens": jax.random.normal(ks[2], (N_META, D_MODEL), f32),
        "rel_bias": 0.5 * jax.random.normal(ks[3], (N_BUCKETS, N_HEADS), f32),
        "norm_ffn1": gain(ks[4], (DEPTH, D_MODEL)),
        "w_ffn1_gate": w(ks[5], (DEPTH, D_MODEL, D_FF), D_MODEL),
        "w_ffn1_up": w(ks[6], (DEPTH, D_MODEL, D_FF), D_MODEL),
        "w_ffn1_down": w(ks[7], (DEPTH, D_FF, D_MODEL), D_FF),
        "norm_mix": gain(ks[8], (DEPTH, D_MODEL)),
        "w_in": w(ks[9], (DEPTH, D_MODEL, D_IN), D_MODEL),
        "lambda_q1": 0.1 * jax.random.normal(ks[10], (DEPTH, HEAD_DIM), f32),
        "lambda_k1": 0.1 * jax.random.normal(ks[11], (DEPTH, HEAD_DIM), f32),
        "lambda_q2": 0.1 * jax.random.normal(ks[12], (DEPTH, HEAD_DIM), f32),
        "lambda_k2": 0.1 * jax.random.normal(ks[13], (DEPTH, HEAD_DIM), f32),
        "subln_g": gain(ks[14], (DEPTH, V_DIM)),
        "w_attn_out": w(ks[15], (DEPTH, ATTN_WIDTH, D_MODEL), ATTN_WIDTH),
        "conv_w": w(ks[16], (DEPTH, CONV_KERNEL, CONV_CH), CONV_KERNEL),
        "conv_b": 0.02 * jax.random.normal(ks[17], (DEPTH, CONV_CH), f32),
        "conv_ln_g": gain(ks[18], (DEPTH, CONV_CH)),
        "conv_ln_b": 0.02 * jax.random.normal(ks[19], (DEPTH, CONV_CH), f32),
        "w_conv_out": w(ks[20], (DEPTH, CONV_CH, D_MODEL), CONV_CH),
        "w_out": w(ks[21], (DEPTH, D_MODEL, D_MODEL), D_MODEL),
        "norm_ffn2": gain(ks[22], (DEPTH, D_MODEL)),
        "w_ffn2_gate": w(ks[23], (DEPTH, D_MODEL, D_FF), D_MODEL),
        "w_ffn2_up": w(ks[24], (DEPTH, D_MODEL, D_FF), D_MODEL),
        "w_ffn2_down": w(ks[25], (DEPTH, D_FF, D_MODEL), D_FF),
        "final_norm": gain(ks[26], (D_MODEL,)),
    }


def reference(x_prompt, x_sample, meta_tokens, rel_bias,
              norm_ffn1, w_ffn1_gate, w_ffn1_up, w_ffn1_down,
              norm_mix, w_in, lambda_q1, lambda_k1, lambda_q2, lambda_k2, subln_g,
              w_attn_out, conv_w, conv_b, conv_ln_g, conv_ln_b, w_conv_out, w_out,
              norm_ffn2, w_ffn2_gate, w_ffn2_up, w_ffn2_down, final_norm):
    y_prompt = _encode(x_prompt, meta_tokens, rel_bias,
                       norm_ffn1, w_ffn1_gate, w_ffn1_up, w_ffn1_down,
                       norm_mix, w_in, lambda_q1, lambda_k1, lambda_q2, lambda_k2, subln_g,
                       w_attn_out, conv_w, conv_b, conv_ln_g, conv_ln_b, w_conv_out, w_out,
                       norm_ffn2, w_ffn2_gate, w_ffn2_up, w_ffn2_down, final_norm)
    y_sample = _encode(x_sample, meta_tokens, rel_bias,
                       norm_ffn1, w_ffn1_gate, w_ffn1_up, w_ffn1_down,
                       norm_mix, w_in, lambda_q1, lambda_k1, lambda_q2, lambda_k2, subln_g,
                       w_attn_out, conv_w, conv_b, conv_ln_g, conv_ln_b, w_conv_out, w_out,
                       norm_ffn2, w_ffn2_gate, w_ffn2_up, w_ffn2_down, final_norm)
    return (y_prompt, y_sample)
```

```python
import functools
import math

import numpy as np
import jax
import jax.numpy as jnp
from jax import lax
from jax.experimental import pallas as pl
from jax.experimental.pallas import tpu as pltpu

F32 = jnp.float32
BF16 = jnp.bfloat16

EPS = 1e-6
LN_EPS = 1e-5
MAX_DISTANCE = 128
LOG2E = math.log2(math.e)

V7X_LANES = 128
V7X_BF16_SUBLANES = 16
V7X_VMEM_LIMIT_BYTES = 56 * 1024 * 1024

_ARB2 = ("arbitrary", "arbitrary")
_ARB3 = ("arbitrary", "arbitrary", "arbitrary")


def _params(sem):
    return pltpu.CompilerParams(dimension_semantics=sem,
                                vmem_limit_bytes=V7X_VMEM_LIMIT_BYTES)


def _divisor_tile(n, target, mult):
    best = None
    for t in range(mult, min(n, target) + 1, mult):
        if n % t == 0:
            best = t
    if best is None:
        raise ValueError(f"no tile for n={n} target={target} mult={mult}")
    return best


def _rmsnorm_rows(h_ref, g_ref, xn_ref):
    rows = V7X_BF16_SUBLANES
    g = g_ref[...]

    def body(i, c):
        r = pl.multiple_of(i * rows, rows)
        x = h_ref[pl.ds(r, rows), :]
        ms = jnp.mean(x * x, axis=-1, keepdims=True)
        xn_ref[pl.ds(r, rows), :] = (x * lax.rsqrt(ms + EPS) * g).astype(BF16)
        return c

    lax.fori_loop(0, h_ref.shape[0] // rows, body, 0)


def _ffn_up_kernel(h_ref, g_ref, wg_ref, wu_ref, o_ref, xn_ref):
    @pl.when(pl.program_id(1) == 0)
    def _():
        _rmsnorm_rows(h_ref, g_ref, xn_ref)

    xn = xn_ref[...]
    a = jnp.dot(xn, wg_ref[...], preferred_element_type=F32)
    b = jnp.dot(xn, wu_ref[...], preferred_element_type=F32)
    o_ref[...] = (a * jax.nn.sigmoid(a) * b).astype(BF16)


def _ffn_up(h, g, wg, wu, tm, tf):
    M, D = h.shape
    F = wg.shape[1]
    return pl.pallas_call(
        _ffn_up_kernel,
        grid=(M // tm, F // tf),
        in_specs=[pl.BlockSpec((tm, D), lambda m, f: (m, 0)),
                  pl.BlockSpec((1, D), lambda m, f: (0, 0)),
                  pl.BlockSpec((D, tf), lambda m, f: (0, f)),
                  pl.BlockSpec((D, tf), lambda m, f: (0, f))],
        out_specs=pl.BlockSpec((tm, tf), lambda m, f: (m, f)),
        out_shape=jax.ShapeDtypeStruct((M, F), BF16),
        scratch_shapes=[pltpu.VMEM((tm, D), BF16)],
        compiler_params=_params(_ARB2),
        name="ffn_up",
    )(h, g, wg, wu)


def _mm_res_kernel(x_ref, w_ref, h_ref, o_ref, *, scale):
    acc = jnp.dot(x_ref[...], w_ref[...], preferred_element_type=F32)
    o_ref[...] = h_ref[...] + scale * acc


def _mm_res(x, w, h, scale, tm, tn):
    M, K = x.shape
    N = w.shape[1]
    return pl.pallas_call(
        functools.partial(_mm_res_kernel, scale=scale),
        grid=(M // tm, N // tn),
        in_specs=[pl.BlockSpec((tm, K), lambda m, n: (m, 0)),
                  pl.BlockSpec((K, tn), lambda m, n: (0, n)),
                  pl.BlockSpec((tm, tn), lambda m, n: (m, n))],
        out_specs=pl.BlockSpec((tm, tn), lambda m, n: (m, n)),
        out_shape=jax.ShapeDtypeStruct((M, N), F32),
        compiler_params=_params(_ARB2),
        name="mm_res",
    )(x, w, h)


def _proj_kernel(h_ref, g_ref, w_ref, o_ref, xn_ref, *, n_q, n_lin, qscale):
    n = pl.program_id(1)

    @pl.when(n == 0)
    def _():
        _rmsnorm_rows(h_ref, g_ref, xn_ref)

    def acc():
        return jnp.dot(xn_ref[...], w_ref[...], preferred_element_type=F32)

    @pl.when(n < n_q)
    def _():
        o_ref[...] = (acc() * qscale).astype(BF16)

    @pl.when(jnp.logical_and(n >= n_q, n < n_lin))
    def _():
        o_ref[...] = acc().astype(BF16)

    @pl.when(n >= n_lin)
    def _():
        o_ref[...] = jax.nn.sigmoid(acc()).astype(BF16)


def _proj(h, g, w, tm, tn, q_width, gate_start, qscale):
    M, D = h.shape
    N = w.shape[1]
    assert q_width % tn == 0 and gate_start % tn == 0
    return pl.pallas_call(
        functools.partial(_proj_kernel, n_q=q_width // tn, n_lin=gate_start // tn,
                          qscale=qscale),
        grid=(M // tm, N // tn),
        in_specs=[pl.BlockSpec((tm, D), lambda m, n: (m, 0)),
                  pl.BlockSpec((1, D), lambda m, n: (0, 0)),
                  pl.BlockSpec((D, tn), lambda m, n: (0, n))],
        out_specs=pl.BlockSpec((tm, tn), lambda m, n: (m, n)),
        out_shape=jax.ShapeDtypeStruct((M, N), BF16),
        scratch_shapes=[pltpu.VMEM((tm, D), BF16)],
        compiler_params=_params(_ARB2),
        name="proj",
    )(h, g, w)


def _t5_bucket(rel, n_buckets):
    nb = n_buckets // 2
    max_exact = nb // 2
    ret = (rel > 0).astype(jnp.int32) * nb
    n = jnp.abs(rel)
    nf = jnp.maximum(n, 1).astype(jnp.float32)
    large = max_exact + (jnp.log(nf / max_exact) / math.log(MAX_DISTANCE / max_exact)
                         * (nb - max_exact)).astype(jnp.int32)
    large = jnp.minimum(large, nb - 1)
    return ret + jnp.where(n < max_exact, n, large)


class _AttnPlan:
    def __init__(self, L, tq_target, tk):
        self.L = L
        self.tk = tk
        self.tq = _divisor_tile(L, tq_target, V7X_BF16_SUBLANES)
        self.nq = L // self.tq
        self.nk = L // tk
        self.n_tail = L - self.nk * tk
        assert self.n_tail % V7X_BF16_SUBLANES == 0
        self.jl = [self.far_left(i * self.tq) for i in range(self.nq)]
        self.jr = [self.far_right(i * self.tq) for i in range(self.nq)]
        self.n_near = max(1, max(r - l for l, r in zip(self.jl, self.jr)))

    def far_left(self, q0):
        return min(max((q0 - MAX_DISTANCE + 1) // self.tk, 0), self.nk)

    def far_right(self, q0):
        return min(-(-(q0 + self.tq - 1 + MAX_DISTANCE) // self.tk), self.nk)


def _bucket_tiles(plan, n_buckets):
    tq, tk = plan.tq, plan.tk
    qpos = (jnp.arange(plan.nq, dtype=jnp.int32)[:, None, None, None] * tq
            + jnp.arange(tq, dtype=jnp.int32)[None, None, :, None])
    jl = jnp.asarray(np.asarray(plan.jl, np.int32))
    kstart = (jl[:, None] + jnp.arange(plan.n_near, dtype=jnp.int32)[None, :]) * tk
    kcol = jnp.arange(tk, dtype=jnp.int32)[None, None, None, :]
    near = _t5_bucket(kstart[:, :, None, None] + kcol - qpos, n_buckets)
    tail = _t5_bucket(plan.nk * tk + kcol - qpos, n_buckets)
    tail = jnp.where(kcol < plan.n_tail, tail, -1)
    return jnp.concatenate([near, tail], axis=1)


def _bank_kernel(rb_ref, bt_ref, o_ref, *, n_buckets):
    h = pl.program_id(0)
    bt = bt_ref[...]
    acc = jnp.full(bt.shape, -jnp.inf, F32)
    for b in range(n_buckets):
        acc = jnp.where(bt == b, rb_ref[b, h] * LOG2E, acc)
    o_ref[...] = acc


def _bias_bank(rel_bias, plan):
    n_buckets, H = rel_bias.shape
    bt = _bucket_tiles(plan, n_buckets)
    nq, nt, tq, tk = bt.shape
    return pl.pallas_call(
        functools.partial(_bank_kernel, n_buckets=n_buckets),
        grid=(H, nq, nt),
        in_specs=[pl.BlockSpec(memory_space=pltpu.SMEM),
                  pl.BlockSpec((None, None, tq, tk), lambda h, i, t: (i, t, 0, 0))],
        out_specs=pl.BlockSpec((None, None, None, tq, tk), lambda h, i, t: (h, i, t, 0, 0)),
        out_shape=jax.ShapeDtypeStruct((H, nq, nt, tq, tk), F32),
        compiler_params=_params(_ARB3),
        name="bias_bank",
    )(rel_bias, bt)


def _attn_kernel(rb_ref, q_ref, k_ref, v_ref, bank_ref, lq1_ref, lk1_ref, lq2_ref, lk2_ref,
                 sg_ref, o_ref, m1, l1, a1, m2, l2, a2, kt, vt, *,
                 tq, tk, nk, n_tail, n_near, dh, lam_init, n_buckets):
    h = pl.program_id(0)
    i = pl.program_id(1)
    V = v_ref.shape[-1]
    q0 = i * tq
    jl = jnp.minimum(jnp.maximum(q0 - MAX_DISTANCE + 1, 0) // tk, nk)
    jr = jnp.minimum((q0 + tq - 1 + MAX_DISTANCE + tk - 1) // tk, nk)
    c_left = rb_ref[n_buckets // 2 - 1, h] * LOG2E
    c_right = rb_ref[n_buckets - 1, h] * LOG2E

    for m_ref, l_ref, a_ref in ((m1, l1, a1), (m2, l2, a2)):
        m_ref[...] = jnp.full(m_ref.shape, -jnp.inf, F32)
        l_ref[...] = jnp.zeros(l_ref.shape, F32)
        a_ref[...] = jnp.zeros(a_ref.shape, F32)

    q1 = q_ref[:, :dh]
    q2 = q_ref[:, dh:]

    def scores(qm, km):
        return lax.dot_general(qm, km, (((1,), (1,)), ((), ())), preferred_element_type=F32)

    def update(s, c, vc, m_ref, l_ref, a_ref):
        m_prev = m_ref[...]
        m_next = jnp.maximum(m_prev, jnp.max(s, axis=1, keepdims=True) + c)
        p = jnp.exp2(s - jnp.tile(m_next - c, (1, tk // V7X_LANES)))
        alpha = jnp.exp2(m_prev - m_next)
        l_ref[...] = alpha * l_ref[...] + jnp.sum(p, axis=1, keepdims=True)
        m_ref[...] = m_next
        pv = jnp.dot(p.astype(BF16), vc, preferred_element_type=F32)
        a_ref[...] = a_ref[...] * jnp.tile(alpha, (1, V // V7X_LANES)) + pv

    def chunk(kc, vc, bias, c):
        s1 = scores(q1, kc[:, :dh])
        s2 = scores(q2, kc[:, dh:])
        if bias is not None:
            s1 = s1 + bias
            s2 = s2 + bias
        update(s1, c, vc, m1, l1, a1)
        update(s2, c, vc, m2, l2, a2)

    def far_body(c):
        def body(j, carry):
            r = pl.multiple_of(j * tk, tk)
            chunk(k_ref[pl.ds(r, tk), :], v_ref[pl.ds(r, tk), :], None, c)
            return carry
        return body

    def near_body(j, carry):
        r = pl.multiple_of(j * tk, tk)
        chunk(k_ref[pl.ds(r, tk), :], v_ref[pl.ds(r, tk), :], bank_ref[j - jl], 0.0)
        return carry

    lax.fori_loop(0, jl, far_body(c_left), 0)
    lax.fori_loop(jl, jr, near_body, 0)
    lax.fori_loop(jr, nk, far_body(c_right), 0)

    if n_tail:
        kt[...] = jnp.zeros(kt.shape, BF16)
        vt[...] = jnp.zeros(vt.shape, BF16)
        kt[0:n_tail, :] = k_ref[nk * tk:nk * tk + n_tail, :]
        vt[0:n_tail, :] = v_ref[nk * tk:nk * tk + n_tail, :]
        chunk(kt[...], vt[...], bank_ref[n_near], 0.0)

    lam = (jnp.exp(jnp.sum(lq1_ref[...] * lk1_ref[...], axis=-1, keepdims=True))
           - jnp.exp(jnp.sum(lq2_ref[...] * lk2_ref[...], axis=-1, keepdims=True))
           + lam_init)
    rep = V // V7X_LANES
    att = (a1[...] / jnp.tile(l1[...], (1, rep))
           - lam * (a2[...] / jnp.tile(l2[...], (1, rep))))
    ms = jnp.mean(att * att, axis=-1, keepdims=True)
    y = att * lax.rsqrt(ms + EPS) * sg_ref[...] * (1.0 - lam_init)
    o_ref[...] = y.astype(BF16)


def _attention(proj3, bank, rel_bias, lq1, lk1, lq2, lk2, subln_g, plan, H, dh, V, lam_init):
    B, L, _ = proj3.shape
    assert V == 2 * dh and V % V7X_LANES == 0 and plan.tk % V7X_LANES == 0
    tq, tk = plan.tq, plan.tk
    nt = plan.n_near + 1
    kern = functools.partial(
        _attn_kernel, tq=tq, tk=tk, nk=plan.nk, n_tail=plan.n_tail, n_near=plan.n_near,
        dh=dh, lam_init=lam_init, n_buckets=rel_bias.shape[0])
    vec = lambda w: pl.BlockSpec((1, w), lambda h, i, b: (0, 0))
    return pl.pallas_call(
        kern,
        grid=(H, plan.nq, B),
        in_specs=[pl.BlockSpec(memory_space=pltpu.SMEM),
                  pl.BlockSpec((None, tq, V), lambda h, i, b: (b, i, h)),
                  pl.BlockSpec((None, L, V), lambda h, i, b: (b, 0, H + h)),
                  pl.BlockSpec((None, L, V), lambda h, i, b: (b, 0, 2 * H + h)),
                  pl.BlockSpec((None, None, nt, tq, tk), lambda h, i, b: (h, i, 0, 0, 0)),
                  vec(dh), vec(dh), vec(dh), vec(dh), vec(V)],
        out_specs=pl.BlockSpec((None, tq, V), lambda h, i, b: (b, i, h)),
        out_shape=jax.ShapeDtypeStruct((B, L, H * V), BF16),
        scratch_shapes=[pltpu.VMEM((tq, V7X_LANES), F32), pltpu.VMEM((tq, V7X_LANES), F32),
                        pltpu.VMEM((tq, V), F32),
                        pltpu.VMEM((tq, V7X_LANES), F32), pltpu.VMEM((tq, V7X_LANES), F32),
                        pltpu.VMEM((tq, V), F32),
                        pltpu.VMEM((tk, V), BF16), pltpu.VMEM((tk, V), BF16)],
        compiler_params=_params(_ARB3),
        name="attn",
    )(rel_bias, proj3, proj3, proj3, bank, lq1, lk1, lq2, lk2, subln_g)


_CONV_HALO = 16
_CONV_ROWS = 16
_CONV_LANES = 256


def _conv_kernel(ua_ref, ub_ref, pa_ref, pb_ref, na_ref, nb_ref, w_ref, cb_ref, lg_ref, lb_ref,
                 o_ref, cpad, y_scr, *, tl, C, KW):
    i = pl.program_id(1)
    last = pl.num_programs(1) - 1
    H0 = _CONV_HALO
    RC = _CONV_ROWS
    off = H0 - KW // 2

    def glu(a, b):
        return a.astype(F32) * jax.nn.sigmoid(b.astype(F32))

    cpad[0:H0, :] = jnp.where(i > 0, glu(pa_ref[...], pb_ref[...]), 0.0)
    cpad[H0 + tl:H0 + tl + H0, :] = jnp.where(i < last, glu(na_ref[...], nb_ref[...]), 0.0)

    def glu_body(c, carry):
        r = pl.multiple_of(c * RC, RC)
        cpad[pl.ds(H0 + r, RC), :] = glu(ua_ref[pl.ds(r, RC), :], ub_ref[pl.ds(r, RC), :])
        return carry

    lax.fori_loop(0, tl // RC, glu_body, 0)

    cg = min(C, _CONV_LANES)
    span = RC + 8 * ((KW + off - 1) // 8)

    def conv_body(c, carry):
        r = pl.multiple_of(c * RC, RC)
        for g in range(C // cg):
            cols = slice(g * cg, (g + 1) * cg)
            win = cpad[pl.ds(r, RC + 2 * H0), cols]
            acc = jnp.zeros((RC, cg), F32)
            for sh in range(8):
                taps = [t for t in range(KW) if (t + off) % 8 == sh]
                if not taps:
                    continue
                wsh = win[sh:sh + span, :]
                for t in taps:
                    a = (t + off) // 8
                    acc = acc + wsh[8 * a:8 * a + RC, :] * w_ref[t:t + 1, cols]
            y_scr[pl.ds(r, RC), cols] = acc + cb_ref[:, cols]
        return carry

    lax.fori_loop(0, tl // RC, conv_body, 0)

    rows = V7X_BF16_SUBLANES

    def ln_body(c, carry):
        r = pl.multiple_of(c * rows, rows)
        x = y_scr[pl.ds(r, rows), :]
        mu = jnp.mean(x, axis=-1, keepdims=True)
        var = jnp.mean(jnp.square(x - mu), axis=-1, keepdims=True)
        y = (x - mu) * lax.rsqrt(var + LN_EPS) * lg_ref[...] + lb_ref[...]
        o_ref[pl.ds(r, rows), :] = (y * jax.nn.sigmoid(y)).astype(BF16)
        return carry

    lax.fori_loop(0, tl // rows, ln_body, 0)


def _conv_module(proj3, conv_w, conv_b, ln_g, ln_b, u_start, tl):
    B, L, _ = proj3.shape
    KW, C = conv_w.shape
    H0 = _CONV_HALO
    assert KW // 2 <= H0 and KW - 1 + H0 - KW // 2 < 2 * H0 + 1
    assert u_start % C == 0 and L % tl == 0 and tl % _CONV_ROWS == 0 and L % H0 == 0
    ca = u_start // C
    hb = tl // H0
    nhb = L // H0
    main = lambda col: pl.BlockSpec((None, tl, C), lambda b, i: (b, i, col))
    prev = lambda col: pl.BlockSpec((None, H0, C), lambda b, i: (b, jnp.maximum(i * hb - 1, 0), col))
    nxt = lambda col: pl.BlockSpec((None, H0, C), lambda b, i: (b, jnp.minimum((i + 1) * hb, nhb - 1), col))
    vec = pl.BlockSpec((1, C), lambda b, i: (0, 0))
    return pl.pallas_call(
        functools.partial(_conv_kernel, tl=tl, C=C, KW=KW),
        grid=(B, L // tl),
        in_specs=[main(ca), main(ca + 1), prev(ca), prev(ca + 1), nxt(ca), nxt(ca + 1),
                  pl.BlockSpec((KW, C), lambda b, i: (0, 0)), vec, vec, vec],
        out_specs=pl.BlockSpec((None, tl, C), lambda b, i: (b, i, 0)),
        out_shape=jax.ShapeDtypeStruct((B, L, C), BF16),
        scratch_shapes=[pltpu.VMEM((tl + 2 * H0, C), F32), pltpu.VMEM((tl, C), F32)],
        compiler_params=_params(_ARB2),
        name="conv",
    )(proj3, proj3, proj3, proj3, proj3, proj3, conv_w, conv_b, ln_g, ln_b)


def _merge_kernel(att_ref, cn_ref, wa_ref, wc_ref, ga_ref, gc_ref, o_ref):
    a = jnp.dot(att_ref[...], wa_ref[...], preferred_element_type=F32)
    c = jnp.dot(cn_ref[...], wc_ref[...], preferred_element_type=F32)
    o_ref[...] = (ga_ref[...].astype(F32) * a + gc_ref[...].astype(F32) * c).astype(BF16)


def _merge(att, cn, wa, wc, proj, gate_start, tm, tn):
    M, A = att.shape
    C = cn.shape[1]
    D = wa.shape[1]
    assert gate_start % tn == 0 and D % tn == 0
    ga0 = gate_start // tn
    gc0 = (gate_start + D) // tn
    return pl.pallas_call(
        _merge_kernel,
        grid=(M // tm, D // tn),
        in_specs=[pl.BlockSpec((tm, A), lambda m, n: (m, 0)),
                  pl.BlockSpec((tm, C), lambda m, n: (m, 0)),
                  pl.BlockSpec((A, tn), lambda m, n: (0, n)),
                  pl.BlockSpec((C, tn), lambda m, n: (0, n)),
                  pl.BlockSpec((tm, tn), lambda m, n: (m, ga0 + n)),
                  pl.BlockSpec((tm, tn), lambda m, n: (m, gc0 + n))],
        out_specs=pl.BlockSpec((tm, tn), lambda m, n: (m, n)),
        out_shape=jax.ShapeDtypeStruct((M, D), BF16),
        compiler_params=_params(_ARB2),
        name="merge",
    )(att, cn, wa, wc, proj, proj)


def _final_norm_kernel(h_ref, g_ref, o_ref):
    x = h_ref[...]
    ms = jnp.mean(x * x, axis=-1, keepdims=True)
    o_ref[...] = x * lax.rsqrt(ms + EPS) * g_ref[...]


def _final_norm(h, g, B, L, n_meta, tr):
    D = h.shape[1]
    S = L - n_meta
    assert L % 8 == 0 and n_meta % 8 == 0 and tr % 8 == 0
    return pl.pallas_call(
        _final_norm_kernel,
        grid=(B, S // tr),
        in_specs=[pl.BlockSpec((pl.Element(tr), pl.Element(D)),
                               lambda b, i: (pl.multiple_of(b * L + n_meta + i * tr, 8), 0)),
                  pl.BlockSpec((1, D), lambda b, i: (0, 0))],
        out_specs=pl.BlockSpec((None, tr, D), lambda b, i: (b, i, 0)),
        out_shape=jax.ShapeDtypeStruct((B, S, D), F32),
        compiler_params=_params(_ARB2),
        name="final_norm",
    )(h, g)


def _tiles(M, L, S, D, F, q_width, gate_start, d_in):
    sub = V7X_BF16_SUBLANES
    return dict(
        tm_norm=_divisor_tile(M, 704, sub),
        tm=_divisor_tile(M, 1024, sub),
        tf=_divisor_tile(F, 512, V7X_LANES),
        tn_res=_divisor_tile(D, 1024, V7X_LANES),
        tn_proj=_divisor_tile(math.gcd(q_width, gate_start, d_in), 512, V7X_LANES),
        tn_merge=_divisor_tile(math.gcd(D, gate_start), 512, V7X_LANES),
        tl=_divisor_tile(L, 1024, _CONV_ROWS),
        tr=_divisor_tile(S, 512, 8),
        tq_target=704 if L <= 4096 else 448,
        tk=256,
    )


def _encode(x, meta, rel_bias, wts, final_norm, dims):
    H, dh, V = dims["H"], dims["dh"], dims["V"]
    B, S, D = x.shape
    n_meta = meta.shape[0]
    L = S + n_meta
    M = B * L
    depth = wts["w_in"].shape[0]
    F = wts["w_ffn1_gate"].shape[-1]
    KW, C = wts["conv_w"].shape[1:]
    d_in = wts["w_in"].shape[-1]
    qk_width = H * 2 * dh
    u_start = 2 * qk_width + H * V
    gate_start = u_start + 2 * C
    assert d_in == gate_start + 2 * D

    t = _tiles(M, L, S, D, F, qk_width, gate_start, d_in)
    plan = _AttnPlan(L, t["tq_target"], t["tk"])
    bank = _bias_bank(rel_bias, plan)
    qscale = dh ** -0.5 * LOG2E

    h = jnp.concatenate([jnp.broadcast_to(meta[None].astype(x.dtype), (B, n_meta, D)), x],
                        axis=1).reshape(M, D)
    row = lambda v: v.reshape(1, -1)
    for l in range(depth):
        lam_init = 0.8 - 0.6 * math.exp(-0.3 * l)
        hm = _ffn_up(h, row(wts["norm_ffn1"][l]), wts["w_ffn1_gate"][l], wts["w_ffn1_up"][l],
                     t["tm_norm"], t["tf"])
        h = _mm_res(hm, wts["w_ffn1_down"][l], h, 0.5, t["tm"], t["tn_res"])

        proj = _proj(h, row(wts["norm_mix"][l]), wts["w_in"][l], t["tm_norm"], t["tn_proj"],
                     qk_width, gate_start, qscale)
        proj3 = proj.reshape(B, L, d_in)
        att = _attention(proj3, bank, rel_bias, row(wts["lambda_q1"][l]), row(wts["lambda_k1"][l]),
                         row(wts["lambda_q2"][l]), row(wts["lambda_k2"][l]), row(wts["subln_g"][l]),
                         plan, H, dh, V, lam_init)
        cn = _conv_module(proj3, wts["conv_w"][l], row(wts["conv_b"][l]), row(wts["conv_ln_g"][l]),
                          row(wts["conv_ln_b"][l]), u_start, t["tl"])
        mg = _merge(att.reshape(M, H * V), cn.reshape(M, C), wts["w_attn_out"][l],
                    wts["w_conv_out"][l], proj, gate_start, t["tm"], t["tn_merge"])
        h = _mm_res(mg, wts["w_out"][l], h, 1.0, t["tm"], t["tn_res"])

        hm = _ffn_up(h, row(wts["norm_ffn2"][l]), wts["w_ffn2_gate"][l], wts["w_ffn2_up"][l],
                     t["tm_norm"], t["tf"])
        h = _mm_res(hm, wts["w_ffn2_down"][l], h, 0.5, t["tm"], t["tn_res"])
    return _final_norm(h, row(final_norm), B, L, n_meta, t["tr"])


_MATMUL_WEIGHTS = ("w_ffn1_gate", "w_ffn1_up", "w_ffn1_down", "w_in", "w_attn_out",
                   "w_conv_out", "w_out", "w_ffn2_gate", "w_ffn2_up", "w_ffn2_down")


def kernel(x_prompt, x_sample, meta_tokens, rel_bias, norm_ffn1, w_ffn1_gate, w_ffn1_up, w_ffn1_down, norm_mix, w_in, lambda_q1, lambda_k1, lambda_q2, lambda_k2, subln_g, w_attn_out, conv_w, conv_b, conv_ln_g, conv_ln_b, w_conv_out, w_out, norm_ffn2, w_ffn2_gate, w_ffn2_up, w_ffn2_down, final_norm):
    wts = dict(norm_ffn1=norm_ffn1, w_ffn1_gate=w_ffn1_gate, w_ffn1_up=w_ffn1_up,
               w_ffn1_down=w_ffn1_down, norm_mix=norm_mix, w_in=w_in, lambda_q1=lambda_q1,
               lambda_k1=lambda_k1, lambda_q2=lambda_q2, lambda_k2=lambda_k2, subln_g=subln_g,
               w_attn_out=w_attn_out, conv_w=conv_w, conv_b=conv_b, conv_ln_g=conv_ln_g,
               conv_ln_b=conv_ln_b, w_conv_out=w_conv_out, w_out=w_out, norm_ffn2=norm_ffn2,
               w_ffn2_gate=w_ffn2_gate, w_ffn2_up=w_ffn2_up, w_ffn2_down=w_ffn2_down)
    for name in _MATMUL_WEIGHTS:
        wts[name] = wts[name].astype(BF16)
    dh = lambda_q1.shape[-1]
    V = subln_g.shape[-1]
    dims = dict(H=rel_bias.shape[1], dh=dh, V=V)
    y_prompt = _encode(x_prompt, meta_tokens, rel_bias, wts, final_norm, dims)
    y_sample = _encode(x_sample, meta_tokens, rel_bias, wts, final_norm, dims)
    return (y_prompt, y_sample)
```

```python
import functools
import math

import numpy as np
import jax
import jax.numpy as jnp
from jax import lax
from jax.experimental import pallas as pl
from jax.experimental.pallas import tpu as pltpu

F32 = jnp.float32
BF16 = jnp.bfloat16

EPS = 1e-6
LN_EPS = 1e-5
MAX_DISTANCE = 128
LOG2E = math.log2(math.e)

V7X_LANES = 128
V7X_BF16_SUBLANES = 16
V7X_VMEM_LIMIT_BYTES = 56 * 1024 * 1024

_ARB2 = ("arbitrary", "arbitrary")
_ARB3 = ("arbitrary", "arbitrary", "arbitrary")


def _params(sem):
    return pltpu.CompilerParams(dimension_semantics=sem,
                                vmem_limit_bytes=V7X_VMEM_LIMIT_BYTES)


def _divisor_tile(n, target, mult):
    best = None
    for t in range(mult, min(n, target) + 1, mult):
        if n % t == 0:
            best = t
    if best is None:
        raise ValueError(f"no tile for n={n} target={target} mult={mult}")
    return best


_NORM_UNROLL = 4


def _rmsnorm_rows(h_ref, g_ref, xn_ref):
    rows = V7X_BF16_SUBLANES
    g = g_ref[...]

    def body(i, c):
        r = pl.multiple_of(i * rows, rows)
        x = h_ref[pl.ds(r, rows), :]
        ms = jnp.mean(x * x, axis=-1, keepdims=True)
        xn_ref[pl.ds(r, rows), :] = (x * lax.rsqrt(ms + EPS) * g).astype(BF16)
        return c

    lax.fori_loop(0, h_ref.shape[0] // rows, body, 0, unroll=_NORM_UNROLL)


def _ffn_up_kernel(h_ref, g_ref, wg_ref, wu_ref, o_ref, xn_ref):
    @pl.when(pl.program_id(1) == 0)
    def _():
        _rmsnorm_rows(h_ref, g_ref, xn_ref)

    xn = xn_ref[...]
    a = jnp.dot(xn, wg_ref[...], preferred_element_type=F32)
    b = jnp.dot(xn, wu_ref[...], preferred_element_type=F32)
    o_ref[...] = (a * jax.nn.sigmoid(a) * b).astype(BF16)


def _ffn_up(h, g, wg, wu, layer, tm, tf):
    M, D = h.shape
    F = wg.shape[2]
    return pl.pallas_call(
        _ffn_up_kernel,
        grid=(M // tm, F // tf),
        in_specs=[pl.BlockSpec((tm, D), lambda m, f: (m, 0)),
                  pl.BlockSpec((1, D), lambda m, f: (0, 0)),
                  pl.BlockSpec((None, D, tf), lambda m, f: (layer, 0, f)),
                  pl.BlockSpec((None, D, tf), lambda m, f: (layer, 0, f))],
        out_specs=pl.BlockSpec((tm, tf), lambda m, f: (m, f)),
        out_shape=jax.ShapeDtypeStruct((M, F), BF16),
        scratch_shapes=[pltpu.VMEM((tm, D), BF16)],
        compiler_params=_params(_ARB2),
        name="ffn_up",
    )(h, g, wg, wu)


def _mm_res_kernel(x_ref, w_ref, h_ref, o_ref, *, scale):
    acc = jnp.dot(x_ref[...], w_ref[...], preferred_element_type=F32)
    o_ref[...] = h_ref[...] + scale * acc


def _mm_res(x, w, layer, h, scale, tm, tn):
    M, K = x.shape
    N = w.shape[2]
    return pl.pallas_call(
        functools.partial(_mm_res_kernel, scale=scale),
        grid=(M // tm, N // tn),
        in_specs=[pl.BlockSpec((tm, K), lambda m, n: (m, 0)),
                  pl.BlockSpec((None, K, tn), lambda m, n: (layer, 0, n)),
                  pl.BlockSpec((tm, tn), lambda m, n: (m, n))],
        out_specs=pl.BlockSpec((tm, tn), lambda m, n: (m, n)),
        out_shape=jax.ShapeDtypeStruct((M, N), F32),
        compiler_params=_params(_ARB2),
        name="mm_res",
    )(x, w, h)


def _proj_kernel(h_ref, g_ref, w_ref, o_ref, xn_ref, *, n_q, n_lin, qscale):
    n = pl.program_id(1)

    @pl.when(n == 0)
    def _():
        _rmsnorm_rows(h_ref, g_ref, xn_ref)

    def acc():
        return jnp.dot(xn_ref[...], w_ref[...], preferred_element_type=F32)

    @pl.when(n < n_q)
    def _():
        o_ref[...] = (acc() * qscale).astype(BF16)

    @pl.when(jnp.logical_and(n >= n_q, n < n_lin))
    def _():
        o_ref[...] = acc().astype(BF16)

    @pl.when(n >= n_lin)
    def _():
        o_ref[...] = jax.nn.sigmoid(acc()).astype(BF16)


def _proj(h, g, w, layer, tm, tn, q_width, gate_start, qscale):
    M, D = h.shape
    N = w.shape[2]
    assert q_width % tn == 0 and gate_start % tn == 0
    return pl.pallas_call(
        functools.partial(_proj_kernel, n_q=q_width // tn, n_lin=gate_start // tn,
                          qscale=qscale),
        grid=(M // tm, N // tn),
        in_specs=[pl.BlockSpec((tm, D), lambda m, n: (m, 0)),
                  pl.BlockSpec((1, D), lambda m, n: (0, 0)),
                  pl.BlockSpec((None, D, tn), lambda m, n: (layer, 0, n))],
        out_specs=pl.BlockSpec((tm, tn), lambda m, n: (m, n)),
        out_shape=jax.ShapeDtypeStruct((M, N), BF16),
        scratch_shapes=[pltpu.VMEM((tm, D), BF16)],
        compiler_params=_params(_ARB2),
        name="proj",
    )(h, g, w)


def _t5_bucket(rel, n_buckets):
    nb = n_buckets // 2
    max_exact = nb // 2
    ret = (rel > 0).astype(jnp.int32) * nb
    n = jnp.abs(rel)
    nf = jnp.maximum(n, 1).astype(jnp.float32)
    large = max_exact + (jnp.log(nf / max_exact) / math.log(MAX_DISTANCE / max_exact)
                         * (nb - max_exact)).astype(jnp.int32)
    large = jnp.minimum(large, nb - 1)
    return ret + jnp.where(n < max_exact, n, large)


_ATTN_TILE_ROWS = 136
_ATTN_UNROLL = 4


class _AttnPlan:
    def __init__(self, L, tq_target, tk):
        self.L = L
        self.tk = tk
        self.tq = _divisor_tile(L, tq_target, V7X_BF16_SUBLANES)
        self.nq = L // self.tq
        self.nk = L // tk
        self.n_tail = L - self.nk * tk
        assert self.n_tail % V7X_BF16_SUBLANES == 0
        self.jl = [self.far_left(i * self.tq) for i in range(self.nq)]
        self.jr = [self.far_right(i * self.tq) for i in range(self.nq)]
        self.n_near = max(1, max(r - l for l, r in zip(self.jl, self.jr)))
        units = self.tq // V7X_BF16_SUBLANES
        n = max(1, round(self.tq / _ATTN_TILE_ROWS))
        sizes = [(units // n + (1 if t < units % n else 0)) * V7X_BF16_SUBLANES for t in range(n)]
        self.row_tiles = tuple((sum(sizes[:t]), sizes[t]) for t in range(n))

    def far_left(self, q0):
        return _near_range(q0, self.tq, self.tk, self.nk, min, max)[0]

    def far_right(self, q0):
        return _near_range(q0, self.tq, self.tk, self.nk, min, max)[1]


def _near_range(q0, tq, tk, nk, minimum, maximum):
    u = _ATTN_UNROLL
    lo = maximum(q0 - MAX_DISTANCE + 1, 0) // tk // u * u
    hi = (q0 + tq - 1 + MAX_DISTANCE + tk - 1) // tk
    hi = minimum((hi + u - 1) // u * u, nk)
    return minimum(lo, nk), hi


def _bucket_tiles(plan, n_buckets):
    tq, tk = plan.tq, plan.tk
    qpos = (jnp.arange(plan.nq, dtype=jnp.int32)[:, None, None, None] * tq
            + jnp.arange(tq, dtype=jnp.int32)[None, None, :, None])
    jl = jnp.asarray(np.asarray(plan.jl, np.int32))
    kstart = (jl[:, None] + jnp.arange(plan.n_near, dtype=jnp.int32)[None, :]) * tk
    kcol = jnp.arange(tk, dtype=jnp.int32)[None, None, None, :]
    near = _t5_bucket(kstart[:, :, None, None] + kcol - qpos, n_buckets)
    tail = _t5_bucket(plan.nk * tk + kcol - qpos, n_buckets)
    tail = jnp.where(kcol < plan.n_tail, tail, -1)
    return jnp.concatenate([near, tail], axis=1)


def _bank_kernel(rb_ref, bt_ref, o_ref, *, n_buckets):
    h = pl.program_id(0)
    bt = bt_ref[...]
    acc = jnp.full(bt.shape, -jnp.inf, F32)
    for b in range(n_buckets):
        acc = jnp.where(bt == b, rb_ref[b, h] * LOG2E, acc)
    o_ref[...] = acc


def _bias_bank(rel_bias, plan):
    n_buckets, H = rel_bias.shape
    bt = _bucket_tiles(plan, n_buckets)
    nq, nt, tq, tk = bt.shape
    return pl.pallas_call(
        functools.partial(_bank_kernel, n_buckets=n_buckets),
        grid=(H, nq, nt),
        in_specs=[pl.BlockSpec(memory_space=pltpu.SMEM),
                  pl.BlockSpec((None, None, tq, tk), lambda h, i, t: (i, t, 0, 0))],
        out_specs=pl.BlockSpec((None, None, None, tq, tk), lambda h, i, t: (h, i, t, 0, 0)),
        out_shape=jax.ShapeDtypeStruct((H, nq, nt, tq, tk), F32),
        compiler_params=_params(_ARB3),
        name="bias_bank",
    )(rel_bias, bt)


def _attn_kernel(rb_ref, q_ref, k_ref, v_ref, bank_ref, lq1_ref, lk1_ref, lq2_ref, lk2_ref,
                 sg_ref, o_ref, m1, l1, a1, m2, l2, a2, kt, vt, *,
                 tq, tk, nk, n_tail, n_near, dh, lam_init, n_buckets, row_tiles):
    h = pl.program_id(0)
    i = pl.program_id(1)
    V = v_ref.shape[-1]
    q0 = i * tq
    jl, jr = _near_range(q0, tq, tk, nk, jnp.minimum, jnp.maximum)
    c_left = rb_ref[n_buckets // 2 - 1, h] * LOG2E
    c_right = rb_ref[n_buckets - 1, h] * LOG2E

    for m_ref, l_ref, a_ref in ((m1, l1, a1), (m2, l2, a2)):
        m_ref[...] = jnp.full(m_ref.shape, -jnp.inf, F32)
        l_ref[...] = jnp.zeros(l_ref.shape, F32)
        a_ref[...] = jnp.zeros(a_ref.shape, F32)

    maps = ((slice(0, dh), m1, l1, a1), (slice(dh, 2 * dh), m2, l2, a2))

    def update(rows, kmap, vc, bias, qcols, m_ref, l_ref, a_ref):
        s = lax.dot_general(q_ref[rows, qcols], kmap, (((1,), (1,)), ((), ())),
                            preferred_element_type=F32)
        if bias is not None:
            s = s + bias
        lane_tiles = s.shape[1] // V7X_LANES
        m_prev = m_ref[rows, :]
        m_next = jnp.maximum(m_prev, jnp.max(s, axis=1, keepdims=True))
        p = jnp.exp2(s - jnp.tile(m_next, (1, lane_tiles)))
        alpha = jnp.exp2(m_prev - m_next)
        psum = p[:, 0:V7X_LANES]
        for t in range(1, lane_tiles):
            psum = psum + p[:, t * V7X_LANES:(t + 1) * V7X_LANES]
        l_ref[rows, :] = alpha * l_ref[rows, :] + psum
        m_ref[rows, :] = m_next
        pv = jnp.dot(p.astype(BF16), vc, preferred_element_type=F32)
        a_ref[rows, :] = a_ref[rows, :] * jnp.tile(alpha, (1, V // V7X_LANES)) + pv

    def chunk(kv_rows, width, bias_tile, k_src, v_src):
        for r0, rq in row_tiles:
            rows = slice(r0, r0 + rq)
            for qcols, m_ref, l_ref, a_ref in maps:
                bias = None if bias_tile is None else bank_ref[bias_tile, rows, 0:width]
                update(rows, k_src[kv_rows, qcols], v_src[kv_rows, :], bias,
                       qcols, m_ref, l_ref, a_ref)

    def far_chunk(j):
        chunk(pl.ds(pl.multiple_of(j * tk, tk), tk), tk, None, k_ref, v_ref)

    def near_chunk(j):
        chunk(pl.ds(pl.multiple_of(j * tk, tk), tk), tk, j - jl, k_ref, v_ref)

    def sweep(lo, hi, one):
        def group(t, carry):
            for u in range(_ATTN_UNROLL):
                one(lo + _ATTN_UNROLL * t + u)
            return carry

        def single(j, carry):
            one(j)
            return carry

        n_groups = (hi - lo) // _ATTN_UNROLL
        lax.fori_loop(0, n_groups, group, 0)
        lax.fori_loop(lo + _ATTN_UNROLL * n_groups, hi, single, 0)

    def shift_max(delta):
        m1[...] = m1[...] + delta
        m2[...] = m2[...] + delta

    sweep(0, jl, far_chunk)
    shift_max(c_left)
    sweep(jl, jr, near_chunk)
    shift_max(-c_right)
    sweep(jr, nk, far_chunk)
    shift_max(c_right)

    if n_tail:
        kt[...] = jnp.zeros(kt.shape, BF16)
        vt[...] = jnp.zeros(vt.shape, BF16)
        kt[0:n_tail, :] = k_ref[nk * tk:nk * tk + n_tail, :]
        vt[0:n_tail, :] = v_ref[nk * tk:nk * tk + n_tail, :]
        chunk(slice(0, kt.shape[0]), kt.shape[0], n_near, kt, vt)

    lam = (jnp.exp(jnp.sum(lq1_ref[...] * lk1_ref[...], axis=-1, keepdims=True))
           - jnp.exp(jnp.sum(lq2_ref[...] * lk2_ref[...], axis=-1, keepdims=True))
           + lam_init)
    for r0, rq in row_tiles:
        rows = slice(r0, r0 + rq)
        d1 = jnp.sum(l1[rows, :], axis=1, keepdims=True)
        d2 = jnp.sum(l2[rows, :], axis=1, keepdims=True)
        att = a1[rows, :] / d1 - lam * (a2[rows, :] / d2)
        ms = jnp.mean(att * att, axis=-1, keepdims=True)
        y = att * lax.rsqrt(ms + EPS) * sg_ref[...] * (1.0 - lam_init)
        o_ref[rows, :] = y.astype(BF16)


def _attention(proj3, bank, rel_bias, lq1, lk1, lq2, lk2, subln_g, plan, H, dh, V, lam_init):
    B, L, _ = proj3.shape
    assert V == 2 * dh and V % V7X_LANES == 0 and plan.tk % V7X_LANES == 0
    tq, tk = plan.tq, plan.tk
    nt = plan.n_near + 1
    tail_w = max(V7X_LANES, -(-plan.n_tail // V7X_LANES) * V7X_LANES)
    kern = functools.partial(
        _attn_kernel, tq=tq, tk=tk, nk=plan.nk, n_tail=plan.n_tail, n_near=plan.n_near,
        dh=dh, lam_init=lam_init, n_buckets=rel_bias.shape[0], row_tiles=plan.row_tiles)
    vec = lambda w: pl.BlockSpec((1, w), lambda h, i, b: (0, 0))
    return pl.pallas_call(
        kern,
        grid=(H, plan.nq, B),
        in_specs=[pl.BlockSpec(memory_space=pltpu.SMEM),
                  pl.BlockSpec((None, tq, V), lambda h, i, b: (b, i, h)),
                  pl.BlockSpec((None, L, V), lambda h, i, b: (b, 0, H + h)),
                  pl.BlockSpec((None, L, V), lambda h, i, b: (b, 0, 2 * H + h)),
                  pl.BlockSpec((None, None, nt, tq, tk), lambda h, i, b: (h, i, 0, 0, 0)),
                  vec(dh), vec(dh), vec(dh), vec(dh), vec(V)],
        out_specs=pl.BlockSpec((None, tq, V), lambda h, i, b: (b, i, h)),
        out_shape=jax.ShapeDtypeStruct((B, L, H * V), BF16),
        scratch_shapes=[pltpu.VMEM((tq, V7X_LANES), F32), pltpu.VMEM((tq, V7X_LANES), F32),
                        pltpu.VMEM((tq, V), F32),
                        pltpu.VMEM((tq, V7X_LANES), F32), pltpu.VMEM((tq, V7X_LANES), F32),
                        pltpu.VMEM((tq, V), F32),
                        pltpu.VMEM((tail_w, V), BF16), pltpu.VMEM((tail_w, V), BF16)],
        compiler_params=_params(_ARB3),
        name="attn",
    )(rel_bias, proj3, proj3, proj3, bank, lq1, lk1, lq2, lk2, subln_g)


_CONV_HALO = 16
_CONV_LANES = 256


def _conv_rows(tl):
    return _divisor_tile(tl, 48, V7X_BF16_SUBLANES)


def _conv_kernel(ua_ref, ub_ref, pa_ref, pb_ref, na_ref, nb_ref, w_ref, cb_ref, lg_ref, lb_ref,
                 o_ref, cpad, y_scr, *, tl, C, KW):
    i = pl.program_id(1)
    last = pl.num_programs(1) - 1
    H0 = _CONV_HALO
    RC = _conv_rows(tl)
    WIN = RC + 2 * H0
    off = H0 - KW // 2

    def glu(a, b):
        return a.astype(F32) * jax.nn.sigmoid(b.astype(F32))

    cpad[0:H0, :] = jnp.where(i > 0, glu(pa_ref[...], pb_ref[...]), 0.0)
    cpad[H0 + tl:H0 + tl + H0, :] = jnp.where(i < last, glu(na_ref[...], nb_ref[...]), 0.0)

    def glu_body(c, carry):
        r = pl.multiple_of(c * RC, RC)
        cpad[pl.ds(H0 + r, RC), :] = glu(ua_ref[pl.ds(r, RC), :], ub_ref[pl.ds(r, RC), :])
        return carry

    lax.fori_loop(0, tl // RC, glu_body, 0)

    cg = min(C, _CONV_LANES)

    def conv_chunk(c):
        r = pl.multiple_of(c * RC, RC)
        for g in range(C // cg):
            cols = slice(g * cg, (g + 1) * cg)
            win = cpad[pl.ds(r, WIN), cols]
            acc = jnp.zeros((RC, cg), F32)
            for sh in range(8):
                taps = [t for t in range(KW) if (t + off) % 8 == sh]
                if not taps:
                    continue
                wsh = win if sh == 0 else pltpu.roll(win, WIN - sh, axis=0)
                for t in taps:
                    a = (t + off) // 8
                    acc = acc + wsh[8 * a:8 * a + RC, :] * w_ref[t:t + 1, cols]
            y_scr[pl.ds(r, RC), cols] = acc + cb_ref[:, cols]

    def ln_chunk(c):
        r = pl.multiple_of(c * RC, RC)
        x = y_scr[pl.ds(r, RC), :]
        mu = jnp.mean(x, axis=-1, keepdims=True)
        var = jnp.mean(jnp.square(x - mu), axis=-1, keepdims=True)
        y = (x - mu) * lax.rsqrt(var + LN_EPS) * lg_ref[...] + lb_ref[...]
        o_ref[pl.ds(r, RC), :] = (y * jax.nn.sigmoid(y)).astype(BF16)

    conv_chunk(0)

    def body(c, carry):
        conv_chunk(c)
        ln_chunk(c - 1)
        return carry

    n_chunks = tl // RC
    lax.fori_loop(1, n_chunks, body, 0)
    ln_chunk(n_chunks - 1)


def _conv_module(proj3, conv_w, conv_b, ln_g, ln_b, u_start, tl):
    B, L, _ = proj3.shape
    KW, C = conv_w.shape
    H0 = _CONV_HALO
    assert KW // 2 <= H0 and KW - 1 + H0 - KW // 2 < 2 * H0 + 1
    assert u_start % C == 0 and L % tl == 0 and tl % V7X_BF16_SUBLANES == 0 and L % H0 == 0
    ca = u_start // C
    hb = tl // H0
    nhb = L // H0
    main = lambda col: pl.BlockSpec((None, tl, C), lambda b, i: (b, i, col))
    prev = lambda col: pl.BlockSpec((None, H0, C), lambda b, i: (b, jnp.maximum(i * hb - 1, 0), col))
    nxt = lambda col: pl.BlockSpec((None, H0, C), lambda b, i: (b, jnp.minimum((i + 1) * hb, nhb - 1), col))
    vec = pl.BlockSpec((1, C), lambda b, i: (0, 0))
    return pl.pallas_call(
        functools.partial(_conv_kernel, tl=tl, C=C, KW=KW),
        grid=(B, L // tl),
        in_specs=[main(ca), main(ca + 1), prev(ca), prev(ca + 1), nxt(ca), nxt(ca + 1),
                  pl.BlockSpec((KW, C), lambda b, i: (0, 0)), vec, vec, vec],
        out_specs=pl.BlockSpec((None, tl, C), lambda b, i: (b, i, 0)),
        out_shape=jax.ShapeDtypeStruct((B, L, C), BF16),
        scratch_shapes=[pltpu.VMEM((tl + 2 * H0, C), F32), pltpu.VMEM((tl, C), F32)],
        compiler_params=_params(_ARB2),
        name="conv",
    )(proj3, proj3, proj3, proj3, proj3, proj3, conv_w, conv_b, ln_g, ln_b)


def _merge_kernel(att_ref, cn_ref, wa_ref, wc_ref, ga_ref, gc_ref, o_ref):
    a = jnp.dot(att_ref[...], wa_ref[...], preferred_element_type=F32)
    c = jnp.dot(cn_ref[...], wc_ref[...], preferred_element_type=F32)
    o_ref[...] = (ga_ref[...].astype(F32) * a + gc_ref[...].astype(F32) * c).astype(BF16)


def _merge(att, cn, wa, wc, layer, proj, gate_start, tm, tn):
    M, A = att.shape
    C = cn.shape[1]
    D = wa.shape[2]
    assert gate_start % tn == 0 and D % tn == 0
    ga0 = gate_start // tn
    gc0 = (gate_start + D) // tn
    return pl.pallas_call(
        _merge_kernel,
        grid=(M // tm, D // tn),
        in_specs=[pl.BlockSpec((tm, A), lambda m, n: (m, 0)),
                  pl.BlockSpec((tm, C), lambda m, n: (m, 0)),
                  pl.BlockSpec((None, A, tn), lambda m, n: (layer, 0, n)),
                  pl.BlockSpec((None, C, tn), lambda m, n: (layer, 0, n)),
                  pl.BlockSpec((tm, tn), lambda m, n: (m, ga0 + n)),
                  pl.BlockSpec((tm, tn), lambda m, n: (m, gc0 + n))],
        out_specs=pl.BlockSpec((tm, tn), lambda m, n: (m, n)),
        out_shape=jax.ShapeDtypeStruct((M, D), BF16),
        compiler_params=_params(_ARB2),
        name="merge",
    )(att, cn, wa, wc, proj, proj)


def _final_norm_kernel(h_ref, g_ref, o_ref):
    x = h_ref[...]
    ms = jnp.mean(x * x, axis=-1, keepdims=True)
    o_ref[...] = x * lax.rsqrt(ms + EPS) * g_ref[...]


def _final_norm(h, g, B, L, n_meta, tr):
    D = h.shape[1]
    S = L - n_meta
    assert L % 8 == 0 and n_meta % 8 == 0 and tr % 8 == 0
    return pl.pallas_call(
        _final_norm_kernel,
        grid=(B, S // tr),
        in_specs=[pl.BlockSpec((pl.Element(tr), pl.Element(D)),
                               lambda b, i: (pl.multiple_of(b * L + n_meta + i * tr, 8), 0)),
                  pl.BlockSpec((1, D), lambda b, i: (0, 0))],
        out_specs=pl.BlockSpec((None, tr, D), lambda b, i: (b, i, 0)),
        out_shape=jax.ShapeDtypeStruct((B, S, D), F32),
        compiler_params=_params(_ARB2),
        name="final_norm",
    )(h, g)


def _conv_tile(L):
    for mult in (48, 32, V7X_BF16_SUBLANES):
        if L % mult == 0:
            return _divisor_tile(L, 2100, mult)
    raise ValueError(f"sequence length {L} is not a multiple of {V7X_BF16_SUBLANES}")


def _tiles(M, L, S, D, F, q_width, gate_start, d_in):
    sub = V7X_BF16_SUBLANES
    return dict(
        tm_norm=_divisor_tile(M, 704, sub),
        tm=_divisor_tile(M, 1024, sub),
        tf=_divisor_tile(F, 512, V7X_LANES),
        tn_res=_divisor_tile(D, 1024, V7X_LANES),
        tn_proj=_divisor_tile(math.gcd(q_width, gate_start, d_in), 1024, V7X_LANES),
        tn_merge=_divisor_tile(math.gcd(D, gate_start), 1024, V7X_LANES),
        tl=_conv_tile(L),
        tr=_divisor_tile(S, 512, 8),
        tq_target=704 if L <= 4096 else 448,
        tk=256,
    )


def _encode(x, meta, rel_bias, wts, final_norm, dims):
    H, dh, V = dims["H"], dims["dh"], dims["V"]
    B, S, D = x.shape
    n_meta = meta.shape[0]
    L = S + n_meta
    M = B * L
    depth = wts["w_in"].shape[0]
    F = wts["w_ffn1_gate"].shape[-1]
    KW, C = wts["conv_w"].shape[1:]
    d_in = wts["w_in"].shape[-1]
    qk_width = H * 2 * dh
    u_start = 2 * qk_width + H * V
    gate_start = u_start + 2 * C
    assert d_in == gate_start + 2 * D

    t = _tiles(M, L, S, D, F, qk_width, gate_start, d_in)
    plan = _AttnPlan(L, t["tq_target"], t["tk"])
    bank = _bias_bank(rel_bias, plan)
    qscale = dh ** -0.5 * LOG2E

    h = jnp.concatenate([jnp.broadcast_to(meta[None].astype(x.dtype), (B, n_meta, D)), x],
                        axis=1).reshape(M, D)
    row = lambda v: v.reshape(1, -1)
    for l in range(depth):
        lam_init = 0.8 - 0.6 * math.exp(-0.3 * l)
        hm = _ffn_up(h, row(wts["norm_ffn1"][l]), wts["w_ffn1_gate"], wts["w_ffn1_up"], l,
                     t["tm_norm"], t["tf"])
        h = _mm_res(hm, wts["w_ffn1_down"], l, h, 0.5, t["tm"], t["tn_res"])

        proj = _proj(h, row(wts["norm_mix"][l]), wts["w_in"], l, t["tm_norm"], t["tn_proj"],
                     qk_width, gate_start, qscale)
        proj3 = proj.reshape(B, L, d_in)
        att = _attention(proj3, bank, rel_bias, row(wts["lambda_q1"][l]), row(wts["lambda_k1"][l]),
                         row(wts["lambda_q2"][l]), row(wts["lambda_k2"][l]), row(wts["subln_g"][l]),
                         plan, H, dh, V, lam_init)
        cn = _conv_module(proj3, wts["conv_w"][l], row(wts["conv_b"][l]), row(wts["conv_ln_g"][l]),
                          row(wts["conv_ln_b"][l]), u_start, t["tl"])
        mg = _merge(att.reshape(M, H * V), cn.reshape(M, C), wts["w_attn_out"],
                    wts["w_conv_out"], l, proj, gate_start, t["tm"], t["tn_merge"])
        h = _mm_res(mg, wts["w_out"], l, h, 1.0, t["tm"], t["tn_res"])

        hm = _ffn_up(h, row(wts["norm_ffn2"][l]), wts["w_ffn2_gate"], wts["w_ffn2_up"], l,
                     t["tm_norm"], t["tf"])
        h = _mm_res(hm, wts["w_ffn2_down"], l, h, 0.5, t["tm"], t["tn_res"])
    return _final_norm(h, row(final_norm), B, L, n_meta, t["tr"])


_MATMUL_WEIGHTS = ("w_ffn1_gate", "w_ffn1_up", "w_ffn1_down", "w_in", "w_attn_out",
                   "w_conv_out", "w_out", "w_ffn2_gate", "w_ffn2_up", "w_ffn2_down")


def kernel(x_prompt, x_sample, meta_tokens, rel_bias, norm_ffn1, w_ffn1_gate, w_ffn1_up, w_ffn1_down, norm_mix, w_in, lambda_q1, lambda_k1, lambda_q2, lambda_k2, subln_g, w_attn_out, conv_w, conv_b, conv_ln_g, conv_ln_b, w_conv_out, w_out, norm_ffn2, w_ffn2_gate, w_ffn2_up, w_ffn2_down, final_norm):
    wts = dict(norm_ffn1=norm_ffn1, w_ffn1_gate=w_ffn1_gate, w_ffn1_up=w_ffn1_up,
               w_ffn1_down=w_ffn1_down, norm_mix=norm_mix, w_in=w_in, lambda_q1=lambda_q1,
               lambda_k1=lambda_k1, lambda_q2=lambda_q2, lambda_k2=lambda_k2, subln_g=subln_g,
               w_attn_out=w_attn_out, conv_w=conv_w, conv_b=conv_b, conv_ln_g=conv_ln_g,
               conv_ln_b=conv_ln_b, w_conv_out=w_conv_out, w_out=w_out, norm_ffn2=norm_ffn2,
               w_ffn2_gate=w_ffn2_gate, w_ffn2_up=w_ffn2_up, w_ffn2_down=w_ffn2_down)
    for name in _MATMUL_WEIGHTS:
        wts[name] = wts[name].astype(BF16)
    dh = lambda_q1.shape[-1]
    V = subln_g.shape[-1]
    dims = dict(H=rel_bias.shape[1], dh=dh, V=V)
    y_prompt = _encode(x_prompt, meta_tokens, rel_bias, wts, final_norm, dims)
    y_sample = _encode(x_sample, meta_tokens, rel_bias, wts, final_norm, dims)
    return (y_prompt, y_sample)
```

```python
import functools
import math

import numpy as np
import jax
import jax.numpy as jnp
from jax import lax
from jax.experimental import pallas as pl
from jax.experimental.pallas import tpu as pltpu

F32 = jnp.float32
BF16 = jnp.bfloat16

EPS = 1e-6
LN_EPS = 1e-5
MAX_DISTANCE = 128
LOG2E = math.log2(math.e)

V7X_LANES = 128
V7X_BF16_SUBLANES = 16
V7X_VMEM_LIMIT_BYTES = 56 * 1024 * 1024

_ARB2 = ("arbitrary", "arbitrary")
_ARB3 = ("arbitrary", "arbitrary", "arbitrary")


def _params(sem):
    return pltpu.CompilerParams(dimension_semantics=sem,
                                vmem_limit_bytes=V7X_VMEM_LIMIT_BYTES)


def _sigmoid(x):
    return 0.5 * jnp.tanh(0.5 * x) + 0.5


def _divisor_tile(n, target, mult):
    best = None
    for t in range(mult, min(n, target) + 1, mult):
        if n % t == 0:
            best = t
    if best is None:
        raise ValueError(f"no tile for n={n} target={target} mult={mult}")
    return best


_NORM_UNROLL = 4


def _rmsnorm_rows(h_ref, g_ref, xn_ref):
    rows = V7X_BF16_SUBLANES
    g = g_ref[...]

    def body(i, c):
        r = pl.multiple_of(i * rows, rows)
        x = h_ref[pl.ds(r, rows), :]
        ms = jnp.mean(x * x, axis=-1, keepdims=True)
        xn_ref[pl.ds(r, rows), :] = (x * lax.rsqrt(ms + EPS) * g).astype(BF16)
        return c

    lax.fori_loop(0, h_ref.shape[0] // rows, body, 0, unroll=_NORM_UNROLL)


def _ffn_up_kernel(h_ref, g_ref, wg_ref, wu_ref, o_ref, xn_ref):
    @pl.when(pl.program_id(1) == 0)
    def _():
        _rmsnorm_rows(h_ref, g_ref, xn_ref)

    xn = xn_ref[...]
    a = jnp.dot(xn, wg_ref[...], preferred_element_type=F32)
    b = jnp.dot(xn, wu_ref[...], preferred_element_type=F32)
    o_ref[...] = (a * _sigmoid(a) * b).astype(BF16)


def _ffn_up(h, g, wg, wu, layer, tm, tf):
    M, D = h.shape
    F = wg.shape[2]
    return pl.pallas_call(
        _ffn_up_kernel,
        grid=(M // tm, F // tf),
        in_specs=[pl.BlockSpec((tm, D), lambda m, f: (m, 0)),
                  pl.BlockSpec((1, D), lambda m, f: (0, 0)),
                  pl.BlockSpec((None, D, tf), lambda m, f: (layer, 0, f)),
                  pl.BlockSpec((None, D, tf), lambda m, f: (layer, 0, f))],
        out_specs=pl.BlockSpec((tm, tf), lambda m, f: (m, f)),
        out_shape=jax.ShapeDtypeStruct((M, F), BF16),
        scratch_shapes=[pltpu.VMEM((tm, D), BF16)],
        compiler_params=_params(_ARB2),
        name="ffn_up",
    )(h, g, wg, wu)


def _mm_res_kernel(x_ref, w_ref, h_ref, o_ref, *, scale):
    acc = jnp.dot(x_ref[...], w_ref[...], preferred_element_type=F32)
    o_ref[...] = h_ref[...] + scale * acc


def _mm_res(x, w, layer, h, scale, tm, tn):
    M, K = x.shape
    N = w.shape[2]
    return pl.pallas_call(
        functools.partial(_mm_res_kernel, scale=scale),
        grid=(M // tm, N // tn),
        in_specs=[pl.BlockSpec((tm, K), lambda m, n: (m, 0)),
                  pl.BlockSpec((None, K, tn), lambda m, n: (layer, 0, n)),
                  pl.BlockSpec((tm, tn), lambda m, n: (m, n))],
        out_specs=pl.BlockSpec((tm, tn), lambda m, n: (m, n)),
        out_shape=jax.ShapeDtypeStruct((M, N), F32),
        compiler_params=_params(_ARB2),
        name="mm_res",
    )(x, w, h)


def _proj_kernel(h_ref, g_ref, w_ref, o_ref, xn_ref, *, n_q, n_lin, qscale):
    n = pl.program_id(1)

    @pl.when(n == 0)
    def _():
        _rmsnorm_rows(h_ref, g_ref, xn_ref)

    def acc():
        return jnp.dot(xn_ref[...], w_ref[...], preferred_element_type=F32)

    @pl.when(n < n_q)
    def _():
        o_ref[...] = (acc() * qscale).astype(BF16)

    @pl.when(jnp.logical_and(n >= n_q, n < n_lin))
    def _():
        o_ref[...] = acc().astype(BF16)

    @pl.when(n >= n_lin)
    def _():
        o_ref[...] = _sigmoid(acc()).astype(BF16)


def _proj(h, g, w, layer, tm, tn, q_width, gate_start, qscale):
    M, D = h.shape
    N = w.shape[2]
    assert q_width % tn == 0 and gate_start % tn == 0
    return pl.pallas_call(
        functools.partial(_proj_kernel, n_q=q_width // tn, n_lin=gate_start // tn,
                          qscale=qscale),
        grid=(M // tm, N // tn),
        in_specs=[pl.BlockSpec((tm, D), lambda m, n: (m, 0)),
                  pl.BlockSpec((1, D), lambda m, n: (0, 0)),
                  pl.BlockSpec((None, D, tn), lambda m, n: (layer, 0, n))],
        out_specs=pl.BlockSpec((tm, tn), lambda m, n: (m, n)),
        out_shape=jax.ShapeDtypeStruct((M, N), BF16),
        scratch_shapes=[pltpu.VMEM((tm, D), BF16)],
        compiler_params=_params(_ARB2),
        name="proj",
    )(h, g, w)


def _t5_bucket(rel, n_buckets):
    nb = n_buckets // 2
    max_exact = nb // 2
    ret = (rel > 0).astype(jnp.int32) * nb
    n = jnp.abs(rel)
    nf = jnp.maximum(n, 1).astype(jnp.float32)
    large = max_exact + (jnp.log(nf / max_exact) / math.log(MAX_DISTANCE / max_exact)
                         * (nb - max_exact)).astype(jnp.int32)
    large = jnp.minimum(large, nb - 1)
    return ret + jnp.where(n < max_exact, n, large)


_ATTN_TILE_ROWS = 136
_ATTN_UNROLL = 4
_ATTN_FAR_UNROLL = 8


class _AttnPlan:
    def __init__(self, L, tq_target, tk):
        self.L = L
        self.tk = tk
        self.tq = _divisor_tile(L, tq_target, V7X_BF16_SUBLANES)
        self.nq = L // self.tq
        self.nk = L // tk
        self.n_tail = L - self.nk * tk
        assert self.n_tail % V7X_BF16_SUBLANES == 0
        self.jl = [self.far_left(i * self.tq) for i in range(self.nq)]
        self.jr = [self.far_right(i * self.tq) for i in range(self.nq)]
        self.n_near = max(1, max(r - l for l, r in zip(self.jl, self.jr)))
        units = self.tq // V7X_BF16_SUBLANES
        n = max(1, round(self.tq / _ATTN_TILE_ROWS))
        sizes = [(units // n + (1 if t < units % n else 0)) * V7X_BF16_SUBLANES for t in range(n)]
        self.row_tiles = tuple((sum(sizes[:t]), sizes[t]) for t in range(n))

    def far_left(self, q0):
        return _near_range(q0, self.tq, self.tk, self.nk, min, max)[0]

    def far_right(self, q0):
        return _near_range(q0, self.tq, self.tk, self.nk, min, max)[1]


def _near_range(q0, tq, tk, nk, minimum, maximum):
    u = _ATTN_UNROLL
    lo = maximum(q0 - MAX_DISTANCE + 1, 0) // tk // u * u
    hi = (q0 + tq - 1 + MAX_DISTANCE + tk - 1) // tk
    hi = minimum((hi + u - 1) // u * u, nk)
    return minimum(lo, nk), hi


def _bucket_tiles(plan, n_buckets):
    tq, tk = plan.tq, plan.tk
    qpos = (jnp.arange(plan.nq, dtype=jnp.int32)[:, None, None, None] * tq
            + jnp.arange(tq, dtype=jnp.int32)[None, None, :, None])
    jl = jnp.asarray(np.asarray(plan.jl, np.int32))
    kstart = (jl[:, None] + jnp.arange(plan.n_near, dtype=jnp.int32)[None, :]) * tk
    kcol = jnp.arange(tk, dtype=jnp.int32)[None, None, None, :]
    near = _t5_bucket(kstart[:, :, None, None] + kcol - qpos, n_buckets)
    tail = _t5_bucket(plan.nk * tk + kcol - qpos, n_buckets)
    tail = jnp.where(kcol < plan.n_tail, tail, -1)
    return jnp.concatenate([near, tail], axis=1)


def _bank_kernel(rb_ref, bt_ref, o_ref, *, n_buckets, tq, tk, nk, n_near):
    i = pl.program_id(0)
    t = pl.program_id(1)
    H = o_ref.shape[0]
    rows = V7X_BF16_SUBLANES
    q0 = i * tq
    lo, _ = _near_range(q0, tq, tk, nk, jnp.minimum, jnp.maximum)
    kstart = (lo + t) * tk
    is_chunk = t < n_near
    all_left = jnp.logical_and(is_chunk, kstart + tk - 1 - q0 <= -MAX_DISTANCE)
    all_right = jnp.logical_and(is_chunk, kstart - (q0 + tq - 1) >= MAX_DISTANCE)

    def fill(bucket):
        for h in range(H):
            o_ref[h] = jnp.full((tq, tk), rb_ref[bucket, h] * LOG2E, F32)

    @pl.when(all_left)
    def _():
        fill(n_buckets // 2 - 1)

    @pl.when(all_right)
    def _():
        fill(n_buckets - 1)

    @pl.when(jnp.logical_not(jnp.logical_or(all_left, all_right)))
    def _():
        def body(c, carry):
            r = pl.multiple_of(c * rows, rows)
            bt = bt_ref[pl.ds(r, rows), :]
            acc = [jnp.full(bt.shape, -jnp.inf, F32) for _ in range(H)]
            for b in range(n_buckets):
                hit = bt == b
                for h in range(H):
                    acc[h] = jnp.where(hit, rb_ref[b, h] * LOG2E, acc[h])
            for h in range(H):
                o_ref[h, pl.ds(r, rows), :] = acc[h]
            return carry

        lax.fori_loop(0, tq // rows, body, 0)


def _bias_bank(rel_bias, plan):
    n_buckets, H = rel_bias.shape
    bt = _bucket_tiles(plan, n_buckets)
    nq, nt, tq, tk = bt.shape
    return pl.pallas_call(
        functools.partial(_bank_kernel, n_buckets=n_buckets, tq=tq, tk=tk, nk=plan.nk,
                          n_near=plan.n_near),
        grid=(nq, nt),
        in_specs=[pl.BlockSpec(memory_space=pltpu.SMEM),
                  pl.BlockSpec((None, None, tq, tk), lambda i, t: (i, t, 0, 0))],
        out_specs=pl.BlockSpec((H, None, None, tq, tk), lambda i, t: (0, i, t, 0, 0)),
        out_shape=jax.ShapeDtypeStruct((H, nq, nt, tq, tk), F32),
        compiler_params=_params(_ARB2),
        name="bias_bank",
    )(rel_bias, bt)


def _attn_kernel(rb_ref, lam0_ref, q_ref, k_ref, v_ref, bank_ref, lq1_ref, lk1_ref, lq2_ref,
                 lk2_ref, sg_ref, o_ref, m1, l1, a1, m2, l2, a2, kt, vt, *,
                 tq, tk, nk, n_tail, n_near, dh, n_buckets, row_tiles):
    h = pl.program_id(0)
    i = pl.program_id(1)
    lam_init = lam0_ref[0]
    out_gain = lam0_ref[1]
    V = v_ref.shape[-1]
    q0 = i * tq
    jl, jr = _near_range(q0, tq, tk, nk, jnp.minimum, jnp.maximum)
    c_left = rb_ref[n_buckets // 2 - 1, h] * LOG2E
    c_right = rb_ref[n_buckets - 1, h] * LOG2E

    for m_ref, l_ref, a_ref in ((m1, l1, a1), (m2, l2, a2)):
        m_ref[...] = jnp.full(m_ref.shape, -jnp.inf, F32)
        l_ref[...] = jnp.zeros(l_ref.shape, F32)
        a_ref[...] = jnp.zeros(a_ref.shape, F32)

    maps = ((slice(0, dh), m1, l1, a1), (slice(dh, 2 * dh), m2, l2, a2))

    def update(rows, kmap, vc, bias, qcols, m_ref, l_ref, a_ref):
        s = lax.dot_general(q_ref[rows, qcols], kmap, (((1,), (1,)), ((), ())),
                            preferred_element_type=F32)
        if bias is not None:
            s = s + bias
        lane_tiles = s.shape[1] // V7X_LANES
        m_prev = m_ref[rows, :]
        m_next = jnp.maximum(m_prev, jnp.max(s, axis=1, keepdims=True))
        p = jnp.exp2(s - jnp.tile(m_next, (1, lane_tiles)))
        alpha = jnp.exp2(m_prev - m_next)
        psum = p[:, 0:V7X_LANES]
        for t in range(1, lane_tiles):
            psum = psum + p[:, t * V7X_LANES:(t + 1) * V7X_LANES]
        l_ref[rows, :] = alpha * l_ref[rows, :] + psum
        m_ref[rows, :] = m_next
        pv = jnp.dot(p.astype(BF16), vc, preferred_element_type=F32)
        a_ref[rows, :] = a_ref[rows, :] * jnp.tile(alpha, (1, V // V7X_LANES)) + pv

    def chunk(kv_rows, width, bias_tile, k_src, v_src):
        for r0, rq in row_tiles:
            rows = slice(r0, r0 + rq)
            for qcols, m_ref, l_ref, a_ref in maps:
                bias = None if bias_tile is None else bank_ref[bias_tile, rows, 0:width]
                update(rows, k_src[kv_rows, qcols], v_src[kv_rows, :], bias,
                       qcols, m_ref, l_ref, a_ref)

    def far_chunk(j):
        chunk(pl.ds(pl.multiple_of(j * tk, tk), tk), tk, None, k_ref, v_ref)

    def near_chunk(j):
        chunk(pl.ds(pl.multiple_of(j * tk, tk), tk), tk, j - jl, k_ref, v_ref)

    def sweep(lo, hi, one, unrolls):
        for unroll in unrolls:
            def group(t, carry, lo=lo, unroll=unroll):
                for u in range(unroll):
                    one(lo + unroll * t + u)
                return carry

            n_groups = (hi - lo) // unroll
            lax.fori_loop(0, n_groups, group, 0)
            lo = lo + unroll * n_groups

    def shift_max(delta):
        m1[...] = m1[...] + delta
        m2[...] = m2[...] + delta

    far_unrolls = (_ATTN_FAR_UNROLL, _ATTN_UNROLL, 1)
    sweep(0, jl, far_chunk, far_unrolls)
    shift_max(c_left)
    sweep(jl, jr, near_chunk, (_ATTN_UNROLL, 1))
    shift_max(-c_right)
    sweep(jr, nk, far_chunk, far_unrolls)
    shift_max(c_right)

    if n_tail:
        kt[...] = jnp.zeros(kt.shape, BF16)
        vt[...] = jnp.zeros(vt.shape, BF16)
        kt[0:n_tail, :] = k_ref[nk * tk:nk * tk + n_tail, :]
        vt[0:n_tail, :] = v_ref[nk * tk:nk * tk + n_tail, :]
        chunk(slice(0, kt.shape[0]), kt.shape[0], n_near, kt, vt)

    lam = (jnp.exp(jnp.sum(lq1_ref[...] * lk1_ref[...], axis=-1, keepdims=True))
           - jnp.exp(jnp.sum(lq2_ref[...] * lk2_ref[...], axis=-1, keepdims=True))
           + lam_init)
    for r0, rq in row_tiles:
        rows = slice(r0, r0 + rq)
        d1 = jnp.sum(l1[rows, :], axis=1, keepdims=True)
        d2 = jnp.sum(l2[rows, :], axis=1, keepdims=True)
        att = a1[rows, :] / d1 - lam * (a2[rows, :] / d2)
        ms = jnp.mean(att * att, axis=-1, keepdims=True)
        y = att * lax.rsqrt(ms + EPS) * sg_ref[...] * out_gain
        o_ref[rows, :] = y.astype(BF16)


def _attention(proj3, bank, rel_bias, lq1, lk1, lq2, lk2, subln_g, plan, H, dh, V, lam_init):
    B, L, _ = proj3.shape
    assert V == 2 * dh and V % V7X_LANES == 0 and plan.tk % V7X_LANES == 0
    tq, tk = plan.tq, plan.tk
    nt = plan.n_near + 1
    tail_w = max(V7X_LANES, -(-plan.n_tail // V7X_LANES) * V7X_LANES)
    kern = functools.partial(
        _attn_kernel, tq=tq, tk=tk, nk=plan.nk, n_tail=plan.n_tail, n_near=plan.n_near,
        dh=dh, n_buckets=rel_bias.shape[0], row_tiles=plan.row_tiles)
    vec = lambda w: pl.BlockSpec((1, w), lambda h, i, b: (0, 0))
    lam0 = jnp.asarray([lam_init, 1.0 - lam_init], F32)
    return pl.pallas_call(
        kern,
        grid=(H, plan.nq, B),
        in_specs=[pl.BlockSpec(memory_space=pltpu.SMEM),
                  pl.BlockSpec(memory_space=pltpu.SMEM),
                  pl.BlockSpec((None, tq, V), lambda h, i, b: (b, i, h)),
                  pl.BlockSpec((None, L, V), lambda h, i, b: (b, 0, H + h)),
                  pl.BlockSpec((None, L, V), lambda h, i, b: (b, 0, 2 * H + h)),
                  pl.BlockSpec((None, None, nt, tq, tk), lambda h, i, b: (h, i, 0, 0, 0)),
                  vec(dh), vec(dh), vec(dh), vec(dh), vec(V)],
        out_specs=pl.BlockSpec((None, tq, V), lambda h, i, b: (b, i, h)),
        out_shape=jax.ShapeDtypeStruct((B, L, H * V), BF16),
        scratch_shapes=[pltpu.VMEM((tq, V7X_LANES), F32), pltpu.VMEM((tq, V7X_LANES), F32),
                        pltpu.VMEM((tq, V), F32),
                        pltpu.VMEM((tq, V7X_LANES), F32), pltpu.VMEM((tq, V7X_LANES), F32),
                        pltpu.VMEM((tq, V), F32),
                        pltpu.VMEM((tail_w, V), BF16), pltpu.VMEM((tail_w, V), BF16)],
        compiler_params=_params(_ARB3),
        name="attn",
    )(rel_bias, lam0, proj3, proj3, proj3, bank, lq1, lk1, lq2, lk2, subln_g)


_CONV_HALO = 16
_CONV_LANES = 256


def _conv_rows(tl):
    return _divisor_tile(tl, 48, V7X_BF16_SUBLANES)


def _conv_kernel(ua_ref, ub_ref, pa_ref, pb_ref, na_ref, nb_ref, w_ref, cb_ref, lg_ref, lb_ref,
                 o_ref, cpad, y_scr, *, tl, C, KW):
    i = pl.program_id(1)
    last = pl.num_programs(1) - 1
    H0 = _CONV_HALO
    RC = _conv_rows(tl)
    WIN = RC + 2 * H0
    off = H0 - KW // 2

    def glu(a, b):
        return a.astype(F32) * _sigmoid(b.astype(F32))

    cpad[0:H0, :] = jnp.where(i > 0, glu(pa_ref[...], pb_ref[...]), 0.0)
    cpad[H0 + tl:H0 + tl + H0, :] = jnp.where(i < last, glu(na_ref[...], nb_ref[...]), 0.0)

    def glu_body(c, carry):
        r = pl.multiple_of(c * RC, RC)
        cpad[pl.ds(H0 + r, RC), :] = glu(ua_ref[pl.ds(r, RC), :], ub_ref[pl.ds(r, RC), :])
        return carry

    lax.fori_loop(0, tl // RC, glu_body, 0)

    cg = min(C, _CONV_LANES)

    def conv_chunk(c):
        r = pl.multiple_of(c * RC, RC)
        for g in range(C // cg):
            cols = slice(g * cg, (g + 1) * cg)
            win = cpad[pl.ds(r, WIN), cols]
            acc = jnp.zeros((RC, cg), F32)
            for sh in range(8):
                taps = [t for t in range(KW) if (t + off) % 8 == sh]
                if not taps:
                    continue
                wsh = win if sh == 0 else pltpu.roll(win, WIN - sh, axis=0)
                for t in taps:
                    a = (t + off) // 8
                    acc = acc + wsh[8 * a:8 * a + RC, :] * w_ref[t:t + 1, cols]
            y_scr[pl.ds(r, RC), cols] = acc + cb_ref[:, cols]

    def ln_chunk(c):
        r = pl.multiple_of(c * RC, RC)
        x = y_scr[pl.ds(r, RC), :]
        mu = jnp.mean(x, axis=-1, keepdims=True)
        var = jnp.mean(jnp.square(x - mu), axis=-1, keepdims=True)
        y = (x - mu) * lax.rsqrt(var + LN_EPS) * lg_ref[...] + lb_ref[...]
        o_ref[pl.ds(r, RC), :] = (y * _sigmoid(y)).astype(BF16)

    conv_chunk(0)

    def body(c, carry):
        conv_chunk(c)
        ln_chunk(c - 1)
        return carry

    n_chunks = tl // RC
    lax.fori_loop(1, n_chunks, body, 0)
    ln_chunk(n_chunks - 1)


def _conv_module(proj3, conv_w, conv_b, ln_g, ln_b, u_start, tl):
    B, L, _ = proj3.shape
    KW, C = conv_w.shape
    H0 = _CONV_HALO
    assert KW // 2 <= H0 and KW - 1 + H0 - KW // 2 < 2 * H0 + 1
    assert u_start % C == 0 and L % tl == 0 and tl % V7X_BF16_SUBLANES == 0 and L % H0 == 0
    ca = u_start // C
    hb = tl // H0
    nhb = L // H0
    main = lambda col: pl.BlockSpec((None, tl, C), lambda b, i: (b, i, col))
    prev = lambda col: pl.BlockSpec((None, H0, C), lambda b, i: (b, jnp.maximum(i * hb - 1, 0), col))
    nxt = lambda col: pl.BlockSpec((None, H0, C), lambda b, i: (b, jnp.minimum((i + 1) * hb, nhb - 1), col))
    vec = pl.BlockSpec((1, C), lambda b, i: (0, 0))
    return pl.pallas_call(
        functools.partial(_conv_kernel, tl=tl, C=C, KW=KW),
        grid=(B, L // tl),
        in_specs=[main(ca), main(ca + 1), prev(ca), prev(ca + 1), nxt(ca), nxt(ca + 1),
                  pl.BlockSpec((KW, C), lambda b, i: (0, 0)), vec, vec, vec],
        out_specs=pl.BlockSpec((None, tl, C), lambda b, i: (b, i, 0)),
        out_shape=jax.ShapeDtypeStruct((B, L, C), BF16),
        scratch_shapes=[pltpu.VMEM((tl + 2 * H0, C), F32), pltpu.VMEM((tl, C), F32)],
        compiler_params=_params(_ARB2),
        name="conv",
    )(proj3, proj3, proj3, proj3, proj3, proj3, conv_w, conv_b, ln_g, ln_b)


def _merge_kernel(att_ref, cn_ref, wa_ref, wc_ref, ga_ref, gc_ref, o_ref):
    a = jnp.dot(att_ref[...], wa_ref[...], preferred_element_type=F32)
    c = jnp.dot(cn_ref[...], wc_ref[...], preferred_element_type=F32)
    o_ref[...] = (ga_ref[...].astype(F32) * a + gc_ref[...].astype(F32) * c).astype(BF16)


def _merge(att, cn, wa, wc, layer, proj, gate_start, tm, tn):
    M, A = att.shape
    C = cn.shape[1]
    D = wa.shape[2]
    assert gate_start % tn == 0 and D % tn == 0
    ga0 = gate_start // tn
    gc0 = (gate_start + D) // tn
    return pl.pallas_call(
        _merge_kernel,
        grid=(M // tm, D // tn),
        in_specs=[pl.BlockSpec((tm, A), lambda m, n: (m, 0)),
                  pl.BlockSpec((tm, C), lambda m, n: (m, 0)),
                  pl.BlockSpec((None, A, tn), lambda m, n: (layer, 0, n)),
                  pl.BlockSpec((None, C, tn), lambda m, n: (layer, 0, n)),
                  pl.BlockSpec((tm, tn), lambda m, n: (m, ga0 + n)),
                  pl.BlockSpec((tm, tn), lambda m, n: (m, gc0 + n))],
        out_specs=pl.BlockSpec((tm, tn), lambda m, n: (m, n)),
        out_shape=jax.ShapeDtypeStruct((M, D), BF16),
        compiler_params=_params(_ARB2),
        name="merge",
    )(att, cn, wa, wc, proj, proj)


def _final_norm_kernel(h_ref, g_ref, o_ref):
    x = h_ref[...]
    ms = jnp.mean(x * x, axis=-1, keepdims=True)
    o_ref[...] = x * lax.rsqrt(ms + EPS) * g_ref[...]


def _final_norm(h, g, B, L, n_meta, tr):
    D = h.shape[1]
    S = L - n_meta
    assert L % 8 == 0 and n_meta % 8 == 0 and tr % 8 == 0
    return pl.pallas_call(
        _final_norm_kernel,
        grid=(B, S // tr),
        in_specs=[pl.BlockSpec((pl.Element(tr), pl.Element(D)),
                               lambda b, i: (pl.multiple_of(b * L + n_meta + i * tr, 8), 0)),
                  pl.BlockSpec((1, D), lambda b, i: (0, 0))],
        out_specs=pl.BlockSpec((None, tr, D), lambda b, i: (b, i, 0)),
        out_shape=jax.ShapeDtypeStruct((B, S, D), F32),
        compiler_params=_params(_ARB2),
        name="final_norm",
    )(h, g)


def _conv_tile(L):
    for mult in (48, 32, V7X_BF16_SUBLANES):
        if L % mult == 0:
            return _divisor_tile(L, 2100, mult)
    raise ValueError(f"sequence length {L} is not a multiple of {V7X_BF16_SUBLANES}")


def _tiles(M, L, S, D, F, q_width, gate_start, d_in):
    sub = V7X_BF16_SUBLANES
    return dict(
        tm_norm=_divisor_tile(M, 704, sub),
        tm=_divisor_tile(M, 1024, sub),
        tf=_divisor_tile(F, 512, V7X_LANES),
        tn_res=_divisor_tile(D, 1024, V7X_LANES),
        tm_down=_divisor_tile(M, 2100, sub),
        tn_down=_divisor_tile(D, 512, V7X_LANES),
        tn_proj=_divisor_tile(math.gcd(q_width, gate_start, d_in), 1024, V7X_LANES),
        tn_merge=_divisor_tile(math.gcd(D, gate_start), 1024, V7X_LANES),
        tl=_conv_tile(L),
        tr=_divisor_tile(S, 512, 8),
        tq_target=704 if L <= 4096 else 448,
        tk=256,
    )


def _encode(x, meta, rel_bias, wts, final_norm, dims):
    H, dh, V = dims["H"], dims["dh"], dims["V"]
    B, S, D = x.shape
    n_meta = meta.shape[0]
    L = S + n_meta
    M = B * L
    depth = wts["w_in"].shape[0]
    F = wts["w_ffn1_gate"].shape[-1]
    KW, C = wts["conv_w"].shape[1:]
    d_in = wts["w_in"].shape[-1]
    qk_width = H * 2 * dh
    u_start = 2 * qk_width + H * V
    gate_start = u_start + 2 * C
    assert d_in == gate_start + 2 * D

    t = _tiles(M, L, S, D, F, qk_width, gate_start, d_in)
    plan = _AttnPlan(L, t["tq_target"], t["tk"])
    bank = _bias_bank(rel_bias, plan)
    qscale = dh ** -0.5 * LOG2E

    h = jnp.concatenate([jnp.broadcast_to(meta[None].astype(x.dtype), (B, n_meta, D)), x],
                        axis=1).reshape(M, D)
    row = lambda v: v.reshape(1, -1)
    for l in range(depth):
        lam_init = 0.8 - 0.6 * math.exp(-0.3 * l)
        hm = _ffn_up(h, row(wts["norm_ffn1"][l]), wts["w_ffn1_gate"], wts["w_ffn1_up"], l,
                     t["tm_norm"], t["tf"])
        h = _mm_res(hm, wts["w_ffn1_down"], l, h, 0.5, t["tm_down"], t["tn_down"])

        proj = _proj(h, row(wts["norm_mix"][l]), wts["w_in"], l, t["tm_norm"], t["tn_proj"],
                     qk_width, gate_start, qscale)
        proj3 = proj.reshape(B, L, d_in)
        att = _attention(proj3, bank, rel_bias, row(wts["lambda_q1"][l]), row(wts["lambda_k1"][l]),
                         row(wts["lambda_q2"][l]), row(wts["lambda_k2"][l]), row(wts["subln_g"][l]),
                         plan, H, dh, V, lam_init)
        cn = _conv_module(proj3, wts["conv_w"][l], row(wts["conv_b"][l]), row(wts["conv_ln_g"][l]),
                          row(wts["conv_ln_b"][l]), u_start, t["tl"])
        mg = _merge(att.reshape(M, H * V), cn.reshape(M, C), wts["w_attn_out"],
                    wts["w_conv_out"], l, proj, gate_start, t["tm"], t["tn_merge"])
        h = _mm_res(mg, wts["w_out"], l, h, 1.0, t["tm"], t["tn_res"])

        hm = _ffn_up(h, row(wts["norm_ffn2"][l]), wts["w_ffn2_gate"], wts["w_ffn2_up"], l,
                     t["tm_norm"], t["tf"])
        h = _mm_res(hm, wts["w_ffn2_down"], l, h, 0.5, t["tm_down"], t["tn_down"])
    return _final_norm(h, row(final_norm), B, L, n_meta, t["tr"])


_MATMUL_WEIGHTS = ("w_ffn1_gate", "w_ffn1_up", "w_ffn1_down", "w_in", "w_attn_out",
                   "w_conv_out", "w_out", "w_ffn2_gate", "w_ffn2_up", "w_ffn2_down")


def kernel(x_prompt, x_sample, meta_tokens, rel_bias, norm_ffn1, w_ffn1_gate, w_ffn1_up, w_ffn1_down, norm_mix, w_in, lambda_q1, lambda_k1, lambda_q2, lambda_k2, subln_g, w_attn_out, conv_w, conv_b, conv_ln_g, conv_ln_b, w_conv_out, w_out, norm_ffn2, w_ffn2_gate, w_ffn2_up, w_ffn2_down, final_norm):
    wts = dict(norm_ffn1=norm_ffn1, w_ffn1_gate=w_ffn1_gate, w_ffn1_up=w_ffn1_up,
               w_ffn1_down=w_ffn1_down, norm_mix=norm_mix, w_in=w_in, lambda_q1=lambda_q1,
               lambda_k1=lambda_k1, lambda_q2=lambda_q2, lambda_k2=lambda_k2, subln_g=subln_g,
               w_attn_out=w_attn_out, conv_w=conv_w, conv_b=conv_b, conv_ln_g=conv_ln_g,
               conv_ln_b=conv_ln_b, w_conv_out=w_conv_out, w_out=w_out, norm_ffn2=norm_ffn2,
               w_ffn2_gate=w_ffn2_gate, w_ffn2_up=w_ffn2_up, w_ffn2_down=w_ffn2_down)
    for name in _MATMUL_WEIGHTS:
        wts[name] = wts[name].astype(BF16)
    dh = lambda_q1.shape[-1]
    V = subln_g.shape[-1]
    dims = dict(H=rel_bias.shape[1], dh=dh, V=V)
    y_prompt = _encode(x_prompt, meta_tokens, rel_bias, wts, final_norm, dims)
    y_sample = _encode(x_sample, meta_tokens, rel_bias, wts, final_norm, dims)
    return (y_prompt, y_sample)
```

```python
import functools
import math

import numpy as np
import jax
import jax.numpy as jnp
from jax import lax
from jax.experimental import pallas as pl
from jax.experimental.pallas import tpu as pltpu

F32 = jnp.float32
BF16 = jnp.bfloat16

EPS = 1e-6
LN_EPS = 1e-5
MAX_DISTANCE = 128
LOG2E = math.log2(math.e)

V7X_LANES = 128
V7X_BF16_SUBLANES = 16
V7X_VMEM_LIMIT_BYTES = 56 * 1024 * 1024
V7X_VMEM_LOOKAHEAD_BYTES = 52 * 1024 * 1024

_ARB2 = ("arbitrary", "arbitrary")
_ARB3 = ("arbitrary", "arbitrary", "arbitrary")


def _params(sem):
    return pltpu.CompilerParams(dimension_semantics=sem,
                                vmem_limit_bytes=V7X_VMEM_LIMIT_BYTES)


def _sigmoid(x):
    return 0.5 * jnp.tanh(0.5 * x) + 0.5


def _divisor_tile(n, target, mult):
    best = None
    for t in range(mult, min(n, target) + 1, mult):
        if n % t == 0:
            best = t
    if best is None:
        raise ValueError(f"no tile for n={n} target={target} mult={mult}")
    return best


_NORM_UNROLL = 4


def _lookahead_rows(n_row_blocks):
    return lambda m, n: (jnp.minimum(m + jnp.minimum(n, 1), n_row_blocks - 1), 0)


def _norm_chunk(h_ref, ssq_ref, g, xn_ref, chunk):
    rows = V7X_BF16_SUBLANES
    r = pl.multiple_of(chunk * rows, rows)
    x = h_ref[pl.ds(r, rows), :]
    d = x.shape[1]
    inv = lax.rsqrt(ssq_ref[pl.ds(r, rows), :] * (1.0 / d) + EPS)
    xn_ref[pl.ds(r, rows), :] = (x * jnp.tile(inv, (1, d // V7X_LANES)) * g).astype(BF16)


def _norm_block(h_ref, ssq_ref, g_ref, xn_ref):
    g = g_ref[...]

    def body(i, c):
        _norm_chunk(h_ref, ssq_ref, g, xn_ref, i)
        return c

    lax.fori_loop(0, h_ref.shape[0] // V7X_BF16_SUBLANES, body, 0, unroll=_NORM_UNROLL)


def _norm_share(h_ref, ssq_ref, g_ref, xn_ref, step, n_steps):
    n_chunks = h_ref.shape[0] // V7X_BF16_SUBLANES
    per_step = -(-n_chunks // n_steps)
    g = g_ref[...]
    for u in range(per_step):
        _norm_chunk(h_ref, ssq_ref, g, xn_ref,
                    jnp.minimum(step * per_step + u, n_chunks - 1))


def _normed_matmul_steps(h_ref, ssq_ref, g_ref, xn_bufs, step_body):
    m = pl.program_id(0)
    n = pl.program_id(1)
    n_steps = pl.num_programs(1) - 1

    if len(xn_bufs) == 1:
        @pl.when(n == 0)
        def _():
            _norm_block(h_ref, ssq_ref, g_ref, xn_bufs[0])

        step_body(xn_bufs[0])
        return

    @pl.when(jnp.logical_and(m == 0, n == 0))
    def _():
        _norm_block(h_ref, ssq_ref, g_ref, xn_bufs[0])
        step_body(xn_bufs[0])

    for parity in (0, 1):
        mine, other = xn_bufs[parity], xn_bufs[1 - parity]

        @pl.when(jnp.logical_and(jnp.logical_and(m > 0, n == 0), m % 2 == parity))
        def _(mine=mine):
            step_body(mine)

        @pl.when(jnp.logical_and(n > 0, m % 2 == parity))
        def _(mine=mine, other=other):
            _norm_share(h_ref, ssq_ref, g_ref, other, n - 1, n_steps)
            step_body(mine)


def _row_ssq_kernel(h_ref, o_ref):
    x = h_ref[...]
    o_ref[...] = jnp.broadcast_to(jnp.sum(x * x, axis=1, keepdims=True), o_ref.shape)


def _row_ssq(h, tr):
    M, D = h.shape
    return pl.pallas_call(
        _row_ssq_kernel,
        grid=(M // tr,),
        in_specs=[pl.BlockSpec((tr, D), lambda m: (m, 0))],
        out_specs=pl.BlockSpec((tr, V7X_LANES), lambda m: (m, 0)),
        out_shape=jax.ShapeDtypeStruct((M, V7X_LANES), F32),
        compiler_params=_params(("arbitrary",)),
        name="row_ssq",
    )(h)


def _ffn_up_kernel(h_ref, ssq_ref, g_ref, wg_ref, wu_ref, o_ref, *xn_bufs):
    def step(xn_ref):
        xn = xn_ref[...]
        a = jnp.dot(xn, wg_ref[...], preferred_element_type=F32)
        b = jnp.dot(xn, wu_ref[...], preferred_element_type=F32)
        o_ref[...] = (a * _sigmoid(a) * b).astype(BF16)

    _normed_matmul_steps(h_ref, ssq_ref, g_ref, xn_bufs, step)


def _ffn_up(h, ssq, g, wg, wu, layer, tm, tf):
    M, D = h.shape
    F = wg.shape[2]
    fixed = 2 * tm * D * 4 + 2 * 2 * D * tf * 2 + 2 * tm * tf * 2 + 4 * tm * tf * 4
    lookahead = F // tf >= 2 and fixed + 2 * tm * D * 2 <= V7X_VMEM_LOOKAHEAD_BYTES
    rows = _lookahead_rows(M // tm) if lookahead else (lambda m, f: (m, 0))
    return pl.pallas_call(
        _ffn_up_kernel,
        grid=(M // tm, F // tf),
        in_specs=[pl.BlockSpec((tm, D), rows),
                  pl.BlockSpec((tm, V7X_LANES), rows),
                  pl.BlockSpec((1, D), lambda m, f: (0, 0)),
                  pl.BlockSpec((None, D, tf), lambda m, f: (layer, 0, f)),
                  pl.BlockSpec((None, D, tf), lambda m, f: (layer, 0, f))],
        out_specs=pl.BlockSpec((tm, tf), lambda m, f: (m, f)),
        out_shape=jax.ShapeDtypeStruct((M, F), BF16),
        scratch_shapes=[pltpu.VMEM((tm, D), BF16)] * (2 if lookahead else 1),
        compiler_params=_params(_ARB2),
        name="ffn_up",
    )(h, ssq, g, wg, wu)


def _mm_res_kernel(x_ref, w_ref, h_ref, o_ref, ssq_ref, *, scale):
    n = pl.program_id(1)
    acc = jnp.dot(x_ref[...], w_ref[...], preferred_element_type=F32)
    y = h_ref[...] + scale * acc
    o_ref[...] = y
    part = jnp.broadcast_to(jnp.sum(y * y, axis=1, keepdims=True), ssq_ref.shape)

    @pl.when(n == 0)
    def _():
        ssq_ref[...] = part

    @pl.when(n > 0)
    def _():
        ssq_ref[...] = ssq_ref[...] + part


def _mm_res(x, w, layer, h, scale, tm, tn):
    M, K = x.shape
    N = w.shape[2]
    return pl.pallas_call(
        functools.partial(_mm_res_kernel, scale=scale),
        grid=(M // tm, N // tn),
        in_specs=[pl.BlockSpec((tm, K), lambda m, n: (m, 0)),
                  pl.BlockSpec((None, K, tn), lambda m, n: (layer, 0, n)),
                  pl.BlockSpec((tm, tn), lambda m, n: (m, n))],
        out_specs=[pl.BlockSpec((tm, tn), lambda m, n: (m, n)),
                   pl.BlockSpec((tm, V7X_LANES), lambda m, n: (m, 0))],
        out_shape=[jax.ShapeDtypeStruct((M, N), F32),
                   jax.ShapeDtypeStruct((M, V7X_LANES), F32)],
        compiler_params=_params(_ARB2),
        name="mm_res",
    )(x, w, h)


def _proj_kernel(h_ref, ssq_ref, g_ref, w_ref, o_ref, xn_ref, *, n_q, n_lin, qscale):
    n = pl.program_id(1)

    @pl.when(n == 0)
    def _():
        _norm_block(h_ref, ssq_ref, g_ref, xn_ref)

    def acc():
        return jnp.dot(xn_ref[...], w_ref[...], preferred_element_type=F32)

    @pl.when(n < n_q)
    def _():
        o_ref[...] = (acc() * qscale).astype(BF16)

    @pl.when(jnp.logical_and(n >= n_q, n < n_lin))
    def _():
        o_ref[...] = acc().astype(BF16)

    @pl.when(n >= n_lin)
    def _():
        o_ref[...] = _sigmoid(acc()).astype(BF16)


def _proj(h, ssq, g, w, layer, tm, tn, q_width, gate_start, qscale):
    M, D = h.shape
    N = w.shape[2]
    assert q_width % tn == 0 and gate_start % tn == 0
    return pl.pallas_call(
        functools.partial(_proj_kernel, n_q=q_width // tn, n_lin=gate_start // tn,
                          qscale=qscale),
        grid=(M // tm, N // tn),
        in_specs=[pl.BlockSpec((tm, D), lambda m, n: (m, 0)),
                  pl.BlockSpec((tm, V7X_LANES), lambda m, n: (m, 0)),
                  pl.BlockSpec((1, D), lambda m, n: (0, 0)),
                  pl.BlockSpec((None, D, tn), lambda m, n: (layer, 0, n))],
        out_specs=pl.BlockSpec((tm, tn), lambda m, n: (m, n)),
        out_shape=jax.ShapeDtypeStruct((M, N), BF16),
        scratch_shapes=[pltpu.VMEM((tm, D), BF16)],
        compiler_params=_params(_ARB2),
        name="proj",
    )(h, ssq, g, w)


def _t5_bucket(rel, n_buckets):
    nb = n_buckets // 2
    max_exact = nb // 2
    ret = (rel > 0).astype(jnp.int32) * nb
    n = jnp.abs(rel)
    nf = jnp.maximum(n, 1).astype(jnp.float32)
    large = max_exact + (jnp.log(nf / max_exact) / math.log(MAX_DISTANCE / max_exact)
                         * (nb - max_exact)).astype(jnp.int32)
    large = jnp.minimum(large, nb - 1)
    return ret + jnp.where(n < max_exact, n, large)


_ATTN_TILE_ROWS = 136
_ATTN_UNROLL = 4
_ATTN_FAR_UNROLL = 8


class _AttnPlan:
    def __init__(self, L, tq_target, tk):
        self.L = L
        self.tk = tk
        self.tq = _divisor_tile(L, tq_target, V7X_BF16_SUBLANES)
        self.nq = L // self.tq
        self.nk = L // tk
        self.n_tail = L - self.nk * tk
        assert self.n_tail % V7X_BF16_SUBLANES == 0
        self.jl = [self.far_left(i * self.tq) for i in range(self.nq)]
        self.jr = [self.far_right(i * self.tq) for i in range(self.nq)]
        self.n_near = max(1, max(r - l for l, r in zip(self.jl, self.jr)))
        units = self.tq // V7X_BF16_SUBLANES
        n = max(1, round(self.tq / _ATTN_TILE_ROWS))
        sizes = [(units // n + (1 if t < units % n else 0)) * V7X_BF16_SUBLANES for t in range(n)]
        self.row_tiles = tuple((sum(sizes[:t]), sizes[t]) for t in range(n))

    def far_left(self, q0):
        return _near_range(q0, self.tq, self.tk, self.nk, min, max)[0]

    def far_right(self, q0):
        return _near_range(q0, self.tq, self.tk, self.nk, min, max)[1]


def _near_range(q0, tq, tk, nk, minimum, maximum):
    u = _ATTN_UNROLL
    lo = maximum(q0 - MAX_DISTANCE + 1, 0) // tk // u * u
    hi = (q0 + tq - 1 + MAX_DISTANCE + tk - 1) // tk
    hi = minimum((hi + u - 1) // u * u, nk)
    return minimum(lo, nk), hi


def _bucket_tiles(plan, n_buckets):
    tq, tk = plan.tq, plan.tk
    qpos = (jnp.arange(plan.nq, dtype=jnp.int32)[:, None, None, None] * tq
            + jnp.arange(tq, dtype=jnp.int32)[None, None, :, None])
    jl = jnp.asarray(np.asarray(plan.jl, np.int32))
    kstart = (jl[:, None] + jnp.arange(plan.n_near, dtype=jnp.int32)[None, :]) * tk
    kcol = jnp.arange(tk, dtype=jnp.int32)[None, None, None, :]
    near = _t5_bucket(kstart[:, :, None, None] + kcol - qpos, n_buckets)
    tail = _t5_bucket(plan.nk * tk + kcol - qpos, n_buckets)
    tail = jnp.where(kcol < plan.n_tail, tail, -1)
    return jnp.concatenate([near, tail], axis=1)


def _bank_kernel(rb_ref, bt_ref, o_ref, *, n_buckets, tq, tk, nk, n_near):
    i = pl.program_id(0)
    t = pl.program_id(1)
    H = o_ref.shape[0]
    rows = V7X_BF16_SUBLANES
    q0 = i * tq
    lo, _ = _near_range(q0, tq, tk, nk, jnp.minimum, jnp.maximum)
    kstart = (lo + t) * tk
    is_chunk = t < n_near
    all_left = jnp.logical_and(is_chunk, kstart + tk - 1 - q0 <= -MAX_DISTANCE)
    all_right = jnp.logical_and(is_chunk, kstart - (q0 + tq - 1) >= MAX_DISTANCE)

    def fill(bucket):
        for h in range(H):
            o_ref[h] = jnp.full((tq, tk), rb_ref[bucket, h] * LOG2E, F32)

    @pl.when(all_left)
    def _():
        fill(n_buckets // 2 - 1)

    @pl.when(all_right)
    def _():
        fill(n_buckets - 1)

    @pl.when(jnp.logical_not(jnp.logical_or(all_left, all_right)))
    def _():
        def body(c, carry):
            r = pl.multiple_of(c * rows, rows)
            bt = bt_ref[pl.ds(r, rows), :]
            acc = [jnp.full(bt.shape, -jnp.inf, F32) for _ in range(H)]
            for b in range(n_buckets):
                hit = bt == b
                for h in range(H):
                    acc[h] = jnp.where(hit, rb_ref[b, h] * LOG2E, acc[h])
            for h in range(H):
                o_ref[h, pl.ds(r, rows), :] = acc[h]
            return carry

        lax.fori_loop(0, tq // rows, body, 0)


def _bias_bank(rel_bias, plan):
    n_buckets, H = rel_bias.shape
    bt = _bucket_tiles(plan, n_buckets)
    nq, nt, tq, tk = bt.shape
    return pl.pallas_call(
        functools.partial(_bank_kernel, n_buckets=n_buckets, tq=tq, tk=tk, nk=plan.nk,
                          n_near=plan.n_near),
        grid=(nq, nt),
        in_specs=[pl.BlockSpec(memory_space=pltpu.SMEM),
                  pl.BlockSpec((None, None, tq, tk), lambda i, t: (i, t, 0, 0))],
        out_specs=pl.BlockSpec((H, None, None, tq, tk), lambda i, t: (0, i, t, 0, 0)),
        out_shape=jax.ShapeDtypeStruct((H, nq, nt, tq, tk), F32),
        compiler_params=_params(_ARB2),
        name="bias_bank",
    )(rel_bias, bt)


def _attn_kernel(rb_ref, lam0_ref, q_ref, k_ref, v_ref, bank_ref, lq1_ref, lk1_ref, lq2_ref,
                 lk2_ref, sg_ref, o_ref, m1, l1, a1, m2, l2, a2, kt, vt, *,
                 tq, tk, nk, n_tail, n_near, dh, n_buckets, row_tiles):
    h = pl.program_id(0)
    i = pl.program_id(1)
    lam_init = lam0_ref[0]
    out_gain = lam0_ref[1]
    V = v_ref.shape[-1]
    q0 = i * tq
    jl, jr = _near_range(q0, tq, tk, nk, jnp.minimum, jnp.maximum)
    c_left = rb_ref[n_buckets // 2 - 1, h] * LOG2E
    c_right = rb_ref[n_buckets - 1, h] * LOG2E

    for m_ref, l_ref, a_ref in ((m1, l1, a1), (m2, l2, a2)):
        m_ref[...] = jnp.full(m_ref.shape, -jnp.inf, F32)
        l_ref[...] = jnp.zeros(l_ref.shape, F32)
        a_ref[...] = jnp.zeros(a_ref.shape, F32)

    maps = ((slice(0, dh), m1, l1, a1), (slice(dh, 2 * dh), m2, l2, a2))

    def update(rows, kmap, vc, bias, qcols, m_ref, l_ref, a_ref):
        s = lax.dot_general(q_ref[rows, qcols], kmap, (((1,), (1,)), ((), ())),
                            preferred_element_type=F32)
        if bias is not None:
            s = s + bias
        lane_tiles = s.shape[1] // V7X_LANES
        m_prev = m_ref[rows, :]
        m_next = jnp.maximum(m_prev, jnp.max(s, axis=1, keepdims=True))
        p = jnp.exp2(s - jnp.tile(m_next, (1, lane_tiles)))
        alpha = jnp.exp2(m_prev - m_next)
        psum = p[:, 0:V7X_LANES]
        for t in range(1, lane_tiles):
            psum = psum + p[:, t * V7X_LANES:(t + 1) * V7X_LANES]
        l_ref[rows, :] = alpha * l_ref[rows, :] + psum
        m_ref[rows, :] = m_next
        pv = jnp.dot(p.astype(BF16), vc, preferred_element_type=F32)
        a_ref[rows, :] = a_ref[rows, :] * jnp.tile(alpha, (1, V // V7X_LANES)) + pv

    def chunk(kv_rows, width, bias_tile, k_src, v_src):
        for r0, rq in row_tiles:
            rows = slice(r0, r0 + rq)
            for qcols, m_ref, l_ref, a_ref in maps:
                bias = None if bias_tile is None else bank_ref[bias_tile, rows, 0:width]
                update(rows, k_src[kv_rows, qcols], v_src[kv_rows, :], bias,
                       qcols, m_ref, l_ref, a_ref)

    def far_chunk(j):
        chunk(pl.ds(pl.multiple_of(j * tk, tk), tk), tk, None, k_ref, v_ref)

    def near_chunk(j):
        chunk(pl.ds(pl.multiple_of(j * tk, tk), tk), tk, j - jl, k_ref, v_ref)

    def sweep(lo, hi, one, unrolls):
        for unroll in unrolls:
            def group(t, carry, lo=lo, unroll=unroll):
                for u in range(unroll):
                    one(lo + unroll * t + u)
                return carry

            n_groups = (hi - lo) // unroll
            lax.fori_loop(0, n_groups, group, 0)
            lo = lo + unroll * n_groups

    def shift_max(delta):
        m1[...] = m1[...] + delta
        m2[...] = m2[...] + delta

    far_unrolls = (_ATTN_FAR_UNROLL, _ATTN_UNROLL, 1)
    sweep(0, jl, far_chunk, far_unrolls)
    shift_max(c_left)
    sweep(jl, jr, near_chunk, (_ATTN_UNROLL, 1))
    shift_max(-c_right)
    sweep(jr, nk, far_chunk, far_unrolls)
    shift_max(c_right)

    if n_tail:
        kt[...] = jnp.zeros(kt.shape, BF16)
        vt[...] = jnp.zeros(vt.shape, BF16)
        kt[0:n_tail, :] = k_ref[nk * tk:nk * tk + n_tail, :]
        vt[0:n_tail, :] = v_ref[nk * tk:nk * tk + n_tail, :]
        chunk(slice(0, kt.shape[0]), kt.shape[0], n_near, kt, vt)

    lam = (jnp.exp(jnp.sum(lq1_ref[...] * lk1_ref[...], axis=-1, keepdims=True))
           - jnp.exp(jnp.sum(lq2_ref[...] * lk2_ref[...], axis=-1, keepdims=True))
           + lam_init)
    for r0, rq in row_tiles:
        rows = slice(r0, r0 + rq)
        d1 = jnp.sum(l1[rows, :], axis=1, keepdims=True)
        d2 = jnp.sum(l2[rows, :], axis=1, keepdims=True)
        att = a1[rows, :] / d1 - lam * (a2[rows, :] / d2)
        ms = jnp.mean(att * att, axis=-1, keepdims=True)
        y = att * lax.rsqrt(ms + EPS) * sg_ref[...] * out_gain
        o_ref[rows, :] = y.astype(BF16)


def _attention(proj3, bank, rel_bias, lq1, lk1, lq2, lk2, subln_g, plan, H, dh, V, lam_init):
    B, L, _ = proj3.shape
    assert V == 2 * dh and V % V7X_LANES == 0 and plan.tk % V7X_LANES == 0
    tq, tk = plan.tq, plan.tk
    nt = plan.n_near + 1
    tail_w = max(V7X_LANES, -(-plan.n_tail // V7X_LANES) * V7X_LANES)
    kern = functools.partial(
        _attn_kernel, tq=tq, tk=tk, nk=plan.nk, n_tail=plan.n_tail, n_near=plan.n_near,
        dh=dh, n_buckets=rel_bias.shape[0], row_tiles=plan.row_tiles)
    vec = lambda w: pl.BlockSpec((1, w), lambda h, i, b: (0, 0))
    lam0 = jnp.asarray([lam_init, 1.0 - lam_init], F32)
    return pl.pallas_call(
        kern,
        grid=(H, plan.nq, B),
        in_specs=[pl.BlockSpec(memory_space=pltpu.SMEM),
                  pl.BlockSpec(memory_space=pltpu.SMEM),
                  pl.BlockSpec((None, tq, V), lambda h, i, b: (b, i, h)),
                  pl.BlockSpec((None, L, V), lambda h, i, b: (b, 0, H + h)),
                  pl.BlockSpec((None, L, V), lambda h, i, b: (b, 0, 2 * H + h)),
                  pl.BlockSpec((None, None, nt, tq, tk), lambda h, i, b: (h, i, 0, 0, 0)),
                  vec(dh), vec(dh), vec(dh), vec(dh), vec(V)],
        out_specs=pl.BlockSpec((None, tq, V), lambda h, i, b: (b, i, h)),
        out_shape=jax.ShapeDtypeStruct((B, L, H * V), BF16),
        scratch_shapes=[pltpu.VMEM((tq, V7X_LANES), F32), pltpu.VMEM((tq, V7X_LANES), F32),
                        pltpu.VMEM((tq, V), F32),
                        pltpu.VMEM((tq, V7X_LANES), F32), pltpu.VMEM((tq, V7X_LANES), F32),
                        pltpu.VMEM((tq, V), F32),
                        pltpu.VMEM((tail_w, V), BF16), pltpu.VMEM((tail_w, V), BF16)],
        compiler_params=_params(_ARB3),
        name="attn",
    )(rel_bias, lam0, proj3, proj3, proj3, bank, lq1, lk1, lq2, lk2, subln_g)


_CONV_HALO = 16
_CONV_LANES = 256


def _conv_rows(tl):
    return _divisor_tile(tl, 48, V7X_BF16_SUBLANES)


def _conv_kernel(ua_ref, ub_ref, pa_ref, pb_ref, na_ref, nb_ref, w_ref, cb_ref, lg_ref, lb_ref,
                 o_ref, cpad, y_scr, *, tl, C, KW):
    i = pl.program_id(1)
    last = pl.num_programs(1) - 1
    H0 = _CONV_HALO
    RC = _conv_rows(tl)
    WIN = RC + 2 * H0
    off = H0 - KW // 2

    def glu(a, b):
        return a.astype(F32) * _sigmoid(b.astype(F32))

    cpad[0:H0, :] = jnp.where(i > 0, glu(pa_ref[...], pb_ref[...]), 0.0)
    cpad[H0 + tl:H0 + tl + H0, :] = jnp.where(i < last, glu(na_ref[...], nb_ref[...]), 0.0)

    def glu_body(c, carry):
        r = pl.multiple_of(c * RC, RC)
        cpad[pl.ds(H0 + r, RC), :] = glu(ua_ref[pl.ds(r, RC), :], ub_ref[pl.ds(r, RC), :])
        return carry

    lax.fori_loop(0, tl // RC, glu_body, 0)

    cg = min(C, _CONV_LANES)

    def conv_chunk(c):
        r = pl.multiple_of(c * RC, RC)
        for g in range(C // cg):
            cols = slice(g * cg, (g + 1) * cg)
            win = cpad[pl.ds(r, WIN), cols]
            acc = jnp.zeros((RC, cg), F32)
            for sh in range(8):
                taps = [t for t in range(KW) if (t + off) % 8 == sh]
                if not taps:
                    continue
                wsh = win if sh == 0 else pltpu.roll(win, WIN - sh, axis=0)
                for t in taps:
                    a = (t + off) // 8
                    acc = acc + wsh[8 * a:8 * a + RC, :] * w_ref[t:t + 1, cols]
            y_scr[pl.ds(r, RC), cols] = acc + cb_ref[:, cols]

    def ln_chunk(c):
        r = pl.multiple_of(c * RC, RC)
        x = y_scr[pl.ds(r, RC), :]
        mu = jnp.mean(x, axis=-1, keepdims=True)
        var = jnp.mean(jnp.square(x - mu), axis=-1, keepdims=True)
        y = (x - mu) * lax.rsqrt(var + LN_EPS) * lg_ref[...] + lb_ref[...]
        o_ref[pl.ds(r, RC), :] = (y * _sigmoid(y)).astype(BF16)

    conv_chunk(0)

    def body(c, carry):
        conv_chunk(c)
        ln_chunk(c - 1)
        return carry

    n_chunks = tl // RC
    lax.fori_loop(1, n_chunks, body, 0)
    ln_chunk(n_chunks - 1)


def _conv_module(proj3, conv_w, conv_b, ln_g, ln_b, u_start, tl):
    B, L, _ = proj3.shape
    KW, C = conv_w.shape
    H0 = _CONV_HALO
    assert KW // 2 <= H0 and KW - 1 + H0 - KW // 2 < 2 * H0 + 1
    assert u_start % C == 0 and L % tl == 0 and tl % V7X_BF16_SUBLANES == 0 and L % H0 == 0
    ca = u_start // C
    hb = tl // H0
    nhb = L // H0
    main = lambda col: pl.BlockSpec((None, tl, C), lambda b, i: (b, i, col))
    prev = lambda col: pl.BlockSpec((None, H0, C), lambda b, i: (b, jnp.maximum(i * hb - 1, 0), col))
    nxt = lambda col: pl.BlockSpec((None, H0, C), lambda b, i: (b, jnp.minimum((i + 1) * hb, nhb - 1), col))
    vec = pl.BlockSpec((1, C), lambda b, i: (0, 0))
    return pl.pallas_call(
        functools.partial(_conv_kernel, tl=tl, C=C, KW=KW),
        grid=(B, L // tl),
        in_specs=[main(ca), main(ca + 1), prev(ca), prev(ca + 1), nxt(ca), nxt(ca + 1),
                  pl.BlockSpec((KW, C), lambda b, i: (0, 0)), vec, vec, vec],
        out_specs=pl.BlockSpec((None, tl, C), lambda b, i: (b, i, 0)),
        out_shape=jax.ShapeDtypeStruct((B, L, C), BF16),
        scratch_shapes=[pltpu.VMEM((tl + 2 * H0, C), F32), pltpu.VMEM((tl, C), F32)],
        compiler_params=_params(_ARB2),
        name="conv",
    )(proj3, proj3, proj3, proj3, proj3, proj3, conv_w, conv_b, ln_g, ln_b)


def _merge_kernel(att_ref, cn_ref, wa_ref, wc_ref, ga_ref, gc_ref, o_ref):
    a = jnp.dot(att_ref[...], wa_ref[...], preferred_element_type=F32)
    c = jnp.dot(cn_ref[...], wc_ref[...], preferred_element_type=F32)
    o_ref[...] = (ga_ref[...].astype(F32) * a + gc_ref[...].astype(F32) * c).astype(BF16)


def _merge(att, cn, wa, wc, layer, proj, gate_start, tm, tn):
    M, A = att.shape
    C = cn.shape[1]
    D = wa.shape[2]
    assert gate_start % tn == 0 and D % tn == 0
    ga0 = gate_start // tn
    gc0 = (gate_start + D) // tn
    return pl.pallas_call(
        _merge_kernel,
        grid=(M // tm, D // tn),
        in_specs=[pl.BlockSpec((tm, A), lambda m, n: (m, 0)),
                  pl.BlockSpec((tm, C), lambda m, n: (m, 0)),
                  pl.BlockSpec((None, A, tn), lambda m, n: (layer, 0, n)),
                  pl.BlockSpec((None, C, tn), lambda m, n: (layer, 0, n)),
                  pl.BlockSpec((tm, tn), lambda m, n: (m, ga0 + n)),
                  pl.BlockSpec((tm, tn), lambda m, n: (m, gc0 + n))],
        out_specs=pl.BlockSpec((tm, tn), lambda m, n: (m, n)),
        out_shape=jax.ShapeDtypeStruct((M, D), BF16),
        compiler_params=_params(_ARB2),
        name="merge",
    )(att, cn, wa, wc, proj, proj)


def _final_norm_kernel(h_ref, g_ref, o_ref):
    x = h_ref[...]
    ms = jnp.mean(x * x, axis=-1, keepdims=True)
    o_ref[...] = x * lax.rsqrt(ms + EPS) * g_ref[...]


def _final_norm(h, g, B, L, n_meta, tr):
    D = h.shape[1]
    S = L - n_meta
    assert L % 8 == 0 and n_meta % 8 == 0 and tr % 8 == 0
    return pl.pallas_call(
        _final_norm_kernel,
        grid=(B, S // tr),
        in_specs=[pl.BlockSpec((pl.Element(tr), pl.Element(D)),
                               lambda b, i: (pl.multiple_of(b * L + n_meta + i * tr, 8), 0)),
                  pl.BlockSpec((1, D), lambda b, i: (0, 0))],
        out_specs=pl.BlockSpec((None, tr, D), lambda b, i: (b, i, 0)),
        out_shape=jax.ShapeDtypeStruct((B, S, D), F32),
        compiler_params=_params(_ARB2),
        name="final_norm",
    )(h, g)


def _conv_tile(L):
    for mult in (48, 32, V7X_BF16_SUBLANES):
        if L % mult == 0:
            return _divisor_tile(L, 2100, mult)
    raise ValueError(f"sequence length {L} is not a multiple of {V7X_BF16_SUBLANES}")


def _tiles(M, L, S, D, F, q_width, gate_start, d_in):
    sub = V7X_BF16_SUBLANES
    return dict(
        tm_norm=_divisor_tile(M, 704, sub),
        tm=_divisor_tile(M, 1024, sub),
        tf=_divisor_tile(F, 512, V7X_LANES),
        tn_res=_divisor_tile(D, 1024, V7X_LANES),
        tm_down=_divisor_tile(M, 2100, sub),
        tn_down=_divisor_tile(D, 512, V7X_LANES),
        tn_proj=_divisor_tile(math.gcd(q_width, gate_start, d_in), 1024, V7X_LANES),
        tn_merge=_divisor_tile(math.gcd(D, gate_start), 1024, V7X_LANES),
        tl=_conv_tile(L),
        tr=_divisor_tile(S, 512, 8),
        tq_target=704 if L <= 4096 else 448,
        tk=256,
    )


def _encode(x, meta, rel_bias, wts, final_norm, dims):
    H, dh, V = dims["H"], dims["dh"], dims["V"]
    B, S, D = x.shape
    n_meta = meta.shape[0]
    L = S + n_meta
    M = B * L
    depth = wts["w_in"].shape[0]
    F = wts["w_ffn1_gate"].shape[-1]
    KW, C = wts["conv_w"].shape[1:]
    d_in = wts["w_in"].shape[-1]
    qk_width = H * 2 * dh
    u_start = 2 * qk_width + H * V
    gate_start = u_start + 2 * C
    assert d_in == gate_start + 2 * D

    t = _tiles(M, L, S, D, F, qk_width, gate_start, d_in)
    plan = _AttnPlan(L, t["tq_target"], t["tk"])
    bank = _bias_bank(rel_bias, plan)
    qscale = dh ** -0.5 * LOG2E

    h = jnp.concatenate([jnp.broadcast_to(meta[None].astype(x.dtype), (B, n_meta, D)), x],
                        axis=1).reshape(M, D)
    ssq = _row_ssq(h, t["tm_norm"])
    row = lambda v: v.reshape(1, -1)
    for l in range(depth):
        lam_init = 0.8 - 0.6 * math.exp(-0.3 * l)
        hm = _ffn_up(h, ssq, row(wts["norm_ffn1"][l]), wts["w_ffn1_gate"], wts["w_ffn1_up"], l,
                     t["tm_norm"], t["tf"])
        h, ssq = _mm_res(hm, wts["w_ffn1_down"], l, h, 0.5, t["tm_down"], t["tn_down"])

        proj = _proj(h, ssq, row(wts["norm_mix"][l]), wts["w_in"], l, t["tm_norm"], t["tn_proj"],
                     qk_width, gate_start, qscale)
        proj3 = proj.reshape(B, L, d_in)
        att = _attention(proj3, bank, rel_bias, row(wts["lambda_q1"][l]), row(wts["lambda_k1"][l]),
                         row(wts["lambda_q2"][l]), row(wts["lambda_k2"][l]), row(wts["subln_g"][l]),
                         plan, H, dh, V, lam_init)
        cn = _conv_module(proj3, wts["conv_w"][l], row(wts["conv_b"][l]), row(wts["conv_ln_g"][l]),
                          row(wts["conv_ln_b"][l]), u_start, t["tl"])
        mg = _merge(att.reshape(M, H * V), cn.reshape(M, C), wts["w_attn_out"],
                    wts["w_conv_out"], l, proj, gate_start, t["tm"], t["tn_merge"])
        h, ssq = _mm_res(mg, wts["w_out"], l, h, 1.0, t["tm"], t["tn_res"])

        hm = _ffn_up(h, ssq, row(wts["norm_ffn2"][l]), wts["w_ffn2_gate"], wts["w_ffn2_up"], l,
                     t["tm_norm"], t["tf"])
        h, ssq = _mm_res(hm, wts["w_ffn2_down"], l, h, 0.5, t["tm_down"], t["tn_down"])
    return _final_norm(h, row(final_norm), B, L, n_meta, t["tr"])


_MATMUL_WEIGHTS = ("w_ffn1_gate", "w_ffn1_up", "w_ffn1_down", "w_in", "w_attn_out",
                   "w_conv_out", "w_out", "w_ffn2_gate", "w_ffn2_up", "w_ffn2_down")


def kernel(x_prompt, x_sample, meta_tokens, rel_bias, norm_ffn1, w_ffn1_gate, w_ffn1_up, w_ffn1_down, norm_mix, w_in, lambda_q1, lambda_k1, lambda_q2, lambda_k2, subln_g, w_attn_out, conv_w, conv_b, conv_ln_g, conv_ln_b, w_conv_out, w_out, norm_ffn2, w_ffn2_gate, w_ffn2_up, w_ffn2_down, final_norm):
    wts = dict(norm_ffn1=norm_ffn1, w_ffn1_gate=w_ffn1_gate, w_ffn1_up=w_ffn1_up,
               w_ffn1_down=w_ffn1_down, norm_mix=norm_mix, w_in=w_in, lambda_q1=lambda_q1,
               lambda_k1=lambda_k1, lambda_q2=lambda_q2, lambda_k2=lambda_k2, subln_g=subln_g,
               w_attn_out=w_attn_out, conv_w=conv_w, conv_b=conv_b, conv_ln_g=conv_ln_g,
               conv_ln_b=conv_ln_b, w_conv_out=w_conv_out, w_out=w_out, norm_ffn2=norm_ffn2,
               w_ffn2_gate=w_ffn2_gate, w_ffn2_up=w_ffn2_up, w_ffn2_down=w_ffn2_down)
    for name in _MATMUL_WEIGHTS:
        wts[name] = wts[name].astype(BF16)
    dh = lambda_q1.shape[-1]
    V = subln_g.shape[-1]
    dims = dict(H=rel_bias.shape[1], dh=dh, V=V)
    y_prompt = _encode(x_prompt, meta_tokens, rel_bias, wts, final_norm, dims)
    y_sample = _encode(x_sample, meta_tokens, rel_bias, wts, final_norm, dims)
    return (y_prompt, y_sample)
```

```python
import functools
import math

import numpy as np
import jax
import jax.numpy as jnp
from jax import lax
from jax.experimental import pallas as pl
from jax.experimental.pallas import tpu as pltpu

F32 = jnp.float32
BF16 = jnp.bfloat16

EPS = 1e-6
LN_EPS = 1e-5
MAX_DISTANCE = 128
LOG2E = math.log2(math.e)

V7X_LANES = 128
V7X_BF16_SUBLANES = 16
V7X_VMEM_LIMIT_BYTES = 56 * 1024 * 1024
V7X_VMEM_LOOKAHEAD_BYTES = 52 * 1024 * 1024

_ARB2 = ("arbitrary", "arbitrary")
_ARB3 = ("arbitrary", "arbitrary", "arbitrary")


def _params(sem):
    return pltpu.CompilerParams(dimension_semantics=sem,
                                vmem_limit_bytes=V7X_VMEM_LIMIT_BYTES)


def _sigmoid(x):
    return 0.5 * jnp.tanh(0.5 * x) + 0.5


def _divisor_tile(n, target, mult):
    best = None
    for t in range(mult, min(n, target) + 1, mult):
        if n % t == 0:
            best = t
    if best is None:
        raise ValueError(f"no tile for n={n} target={target} mult={mult}")
    return best


_NORM_UNROLL = 4


def _lookahead_rows(n_row_blocks):
    return lambda m, n: (jnp.minimum(m + jnp.minimum(n, 1), n_row_blocks - 1), 0)


def _norm_chunk(h_ref, ssq_ref, g, xn_ref, chunk):
    rows = V7X_BF16_SUBLANES
    r = pl.multiple_of(chunk * rows, rows)
    x = h_ref[pl.ds(r, rows), :]
    ms = jnp.sum(ssq_ref[pl.ds(r, rows), :], axis=1, keepdims=True) * (1.0 / x.shape[1])
    xn_ref[pl.ds(r, rows), :] = (x * lax.rsqrt(ms + EPS) * g).astype(BF16)


def _norm_block(h_ref, ssq_ref, g_ref, xn_ref):
    g = g_ref[...]

    def body(i, c):
        _norm_chunk(h_ref, ssq_ref, g, xn_ref, i)
        return c

    lax.fori_loop(0, h_ref.shape[0] // V7X_BF16_SUBLANES, body, 0, unroll=_NORM_UNROLL)


def _norm_share(h_ref, ssq_ref, g_ref, xn_ref, step, n_steps):
    n_chunks = h_ref.shape[0] // V7X_BF16_SUBLANES
    per_step = -(-n_chunks // n_steps)
    g = g_ref[...]
    for u in range(per_step):
        _norm_chunk(h_ref, ssq_ref, g, xn_ref,
                    jnp.minimum(step * per_step + u, n_chunks - 1))


def _normed_matmul_steps(h_ref, ssq_ref, g_ref, xn_bufs, step_body):
    m = pl.program_id(0)
    n = pl.program_id(1)
    n_steps = pl.num_programs(1) - 1

    if len(xn_bufs) == 1:
        @pl.when(n == 0)
        def _():
            _norm_block(h_ref, ssq_ref, g_ref, xn_bufs[0])

        step_body(xn_bufs[0])
        return

    @pl.when(jnp.logical_and(m == 0, n == 0))
    def _():
        _norm_block(h_ref, ssq_ref, g_ref, xn_bufs[0])
        step_body(xn_bufs[0])

    for parity in (0, 1):
        mine, other = xn_bufs[parity], xn_bufs[1 - parity]

        @pl.when(jnp.logical_and(jnp.logical_and(m > 0, n == 0), m % 2 == parity))
        def _(mine=mine):
            step_body(mine)

        @pl.when(jnp.logical_and(n > 0, m % 2 == parity))
        def _(mine=mine, other=other):
            _norm_share(h_ref, ssq_ref, g_ref, other, n - 1, n_steps)
            step_body(mine)


def _row_ssq_kernel(h_ref, o_ref):
    x = h_ref[...]
    sq = x * x
    part = sq[:, 0:V7X_LANES]
    for t in range(1, x.shape[1] // V7X_LANES):
        part = part + sq[:, t * V7X_LANES:(t + 1) * V7X_LANES]
    o_ref[...] = part


def _row_ssq(h, tr):
    M, D = h.shape
    return pl.pallas_call(
        _row_ssq_kernel,
        grid=(M // tr,),
        in_specs=[pl.BlockSpec((tr, D), lambda m: (m, 0))],
        out_specs=pl.BlockSpec((tr, V7X_LANES), lambda m: (m, 0)),
        out_shape=jax.ShapeDtypeStruct((M, V7X_LANES), F32),
        compiler_params=_params(("arbitrary",)),
        name="row_ssq",
    )(h)


def _ffn_up_kernel(h_ref, ssq_ref, g_ref, wg_ref, wu_ref, o_ref, *xn_bufs):
    def step(xn_ref):
        xn = xn_ref[...]
        a = jnp.dot(xn, wg_ref[...], preferred_element_type=F32)
        b = jnp.dot(xn, wu_ref[...], preferred_element_type=F32)
        o_ref[...] = (a * _sigmoid(a) * b).astype(BF16)

    _normed_matmul_steps(h_ref, ssq_ref, g_ref, xn_bufs, step)


def _ffn_up(h, ssq, g, wg, wu, layer, tm, tf):
    M, D = h.shape
    F = wg.shape[2]
    fixed = 2 * tm * D * 4 + 2 * 2 * D * tf * 2 + 2 * tm * tf * 2 + 4 * tm * tf * 4
    lookahead = F // tf >= 2 and fixed + 2 * tm * D * 2 <= V7X_VMEM_LOOKAHEAD_BYTES
    rows = _lookahead_rows(M // tm) if lookahead else (lambda m, f: (m, 0))
    return pl.pallas_call(
        _ffn_up_kernel,
        grid=(M // tm, F // tf),
        in_specs=[pl.BlockSpec((tm, D), rows),
                  pl.BlockSpec((tm, V7X_LANES), rows),
                  pl.BlockSpec((1, D), lambda m, f: (0, 0)),
                  pl.BlockSpec((None, D, tf), lambda m, f: (layer, 0, f)),
                  pl.BlockSpec((None, D, tf), lambda m, f: (layer, 0, f))],
        out_specs=pl.BlockSpec((tm, tf), lambda m, f: (m, f)),
        out_shape=jax.ShapeDtypeStruct((M, F), BF16),
        scratch_shapes=[pltpu.VMEM((tm, D), BF16)] * (2 if lookahead else 1),
        compiler_params=_params(_ARB2),
        name="ffn_up",
    )(h, ssq, g, wg, wu)


def _mm_res_kernel(x_ref, w_ref, h_ref, o_ref, ssq_ref, *, scale):
    @pl.when(pl.program_id(1) == 0)
    def _():
        ssq_ref[...] = jnp.zeros(ssq_ref.shape, F32)

    acc = jnp.dot(x_ref[...], w_ref[...], preferred_element_type=F32)
    y = h_ref[...] + scale * acc
    o_ref[...] = y
    sq = y * y
    part = sq[:, 0:V7X_LANES]
    for t in range(1, y.shape[1] // V7X_LANES):
        part = part + sq[:, t * V7X_LANES:(t + 1) * V7X_LANES]
    ssq_ref[...] = ssq_ref[...] + part


def _mm_res(x, w, layer, h, scale, tm, tn):
    M, K = x.shape
    N = w.shape[2]
    return pl.pallas_call(
        functools.partial(_mm_res_kernel, scale=scale),
        grid=(M // tm, N // tn),
        in_specs=[pl.BlockSpec((tm, K), lambda m, n: (m, 0)),
                  pl.BlockSpec((None, K, tn), lambda m, n: (layer, 0, n)),
                  pl.BlockSpec((tm, tn), lambda m, n: (m, n))],
        out_specs=[pl.BlockSpec((tm, tn), lambda m, n: (m, n)),
                   pl.BlockSpec((tm, V7X_LANES), lambda m, n: (m, 0))],
        out_shape=[jax.ShapeDtypeStruct((M, N), F32),
                   jax.ShapeDtypeStruct((M, V7X_LANES), F32)],
        compiler_params=_params(_ARB2),
        name="mm_res",
    )(x, w, h)


def _proj_kernel(h_ref, ssq_ref, g_ref, w_ref, o_ref, xn_ref, *, n_q, n_lin, qscale):
    n = pl.program_id(1)

    @pl.when(n == 0)
    def _():
        _norm_block(h_ref, ssq_ref, g_ref, xn_ref)

    def acc():
        return jnp.dot(xn_ref[...], w_ref[...], preferred_element_type=F32)

    @pl.when(n < n_q)
    def _():
        o_ref[...] = (acc() * qscale).astype(BF16)

    @pl.when(jnp.logical_and(n >= n_q, n < n_lin))
    def _():
        o_ref[...] = acc().astype(BF16)

    @pl.when(n >= n_lin)
    def _():
        o_ref[...] = _sigmoid(acc()).astype(BF16)


def _proj(h, ssq, g, w, layer, tm, tn, q_width, gate_start, qscale):
    M, D = h.shape
    N = w.shape[2]
    assert q_width % tn == 0 and gate_start % tn == 0
    return pl.pallas_call(
        functools.partial(_proj_kernel, n_q=q_width // tn, n_lin=gate_start // tn,
                          qscale=qscale),
        grid=(M // tm, N // tn),
        in_specs=[pl.BlockSpec((tm, D), lambda m, n: (m, 0)),
                  pl.BlockSpec((tm, V7X_LANES), lambda m, n: (m, 0)),
                  pl.BlockSpec((1, D), lambda m, n: (0, 0)),
                  pl.BlockSpec((None, D, tn), lambda m, n: (layer, 0, n))],
        out_specs=pl.BlockSpec((tm, tn), lambda m, n: (m, n)),
        out_shape=jax.ShapeDtypeStruct((M, N), BF16),
        scratch_shapes=[pltpu.VMEM((tm, D), BF16)],
        compiler_params=_params(_ARB2),
        name="proj",
    )(h, ssq, g, w)


def _t5_bucket(rel, n_buckets):
    nb = n_buckets // 2
    max_exact = nb // 2
    ret = (rel > 0).astype(jnp.int32) * nb
    n = jnp.abs(rel)
    nf = jnp.maximum(n, 1).astype(jnp.float32)
    large = max_exact + (jnp.log(nf / max_exact) / math.log(MAX_DISTANCE / max_exact)
                         * (nb - max_exact)).astype(jnp.int32)
    large = jnp.minimum(large, nb - 1)
    return ret + jnp.where(n < max_exact, n, large)


_ATTN_TILE_ROWS = 136
_ATTN_UNROLL = 4
_ATTN_FAR_UNROLL = 8


class _AttnPlan:
    def __init__(self, L, tq_target, tk):
        self.L = L
        self.tk = tk
        self.tq = _divisor_tile(L, tq_target, V7X_BF16_SUBLANES)
        self.nq = L // self.tq
        self.nk = L // tk
        self.n_tail = L - self.nk * tk
        assert self.n_tail % V7X_BF16_SUBLANES == 0
        self.jl = [self.far_left(i * self.tq) for i in range(self.nq)]
        self.jr = [self.far_right(i * self.tq) for i in range(self.nq)]
        self.n_near = max(1, max(r - l for l, r in zip(self.jl, self.jr)))
        units = self.tq // V7X_BF16_SUBLANES
        n = max(1, round(self.tq / _ATTN_TILE_ROWS))
        sizes = [(units // n + (1 if t < units % n else 0)) * V7X_BF16_SUBLANES for t in range(n)]
        self.row_tiles = tuple((sum(sizes[:t]), sizes[t]) for t in range(n))

    def far_left(self, q0):
        return _near_range(q0, self.tq, self.tk, self.nk, min, max)[0]

    def far_right(self, q0):
        return _near_range(q0, self.tq, self.tk, self.nk, min, max)[1]


def _near_range(q0, tq, tk, nk, minimum, maximum):
    u = _ATTN_UNROLL
    lo = maximum(q0 - MAX_DISTANCE + 1, 0) // tk // u * u
    hi = (q0 + tq - 1 + MAX_DISTANCE + tk - 1) // tk
    hi = minimum((hi + u - 1) // u * u, nk)
    return minimum(lo, nk), hi


def _bucket_tiles(plan, n_buckets):
    tq, tk = plan.tq, plan.tk
    qpos = (jnp.arange(plan.nq, dtype=jnp.int32)[:, None, None, None] * tq
            + jnp.arange(tq, dtype=jnp.int32)[None, None, :, None])
    jl = jnp.asarray(np.asarray(plan.jl, np.int32))
    kstart = (jl[:, None] + jnp.arange(plan.n_near, dtype=jnp.int32)[None, :]) * tk
    kcol = jnp.arange(tk, dtype=jnp.int32)[None, None, None, :]
    near = _t5_bucket(kstart[:, :, None, None] + kcol - qpos, n_buckets)
    tail = _t5_bucket(plan.nk * tk + kcol - qpos, n_buckets)
    tail = jnp.where(kcol < plan.n_tail, tail, -1)
    return jnp.concatenate([near, tail], axis=1)


def _bank_kernel(rb_ref, bt_ref, o_ref, *, n_buckets, tq, tk, nk, n_near):
    i = pl.program_id(0)
    t = pl.program_id(1)
    H = o_ref.shape[0]
    rows = V7X_BF16_SUBLANES
    q0 = i * tq
    lo, _ = _near_range(q0, tq, tk, nk, jnp.minimum, jnp.maximum)
    kstart = (lo + t) * tk
    is_chunk = t < n_near
    all_left = jnp.logical_and(is_chunk, kstart + tk - 1 - q0 <= -MAX_DISTANCE)
    all_right = jnp.logical_and(is_chunk, kstart - (q0 + tq - 1) >= MAX_DISTANCE)

    def fill(bucket):
        for h in range(H):
            o_ref[h] = jnp.full((tq, tk), rb_ref[bucket, h] * LOG2E, F32)

    @pl.when(all_left)
    def _():
        fill(n_buckets // 2 - 1)

    @pl.when(all_right)
    def _():
        fill(n_buckets - 1)

    @pl.when(jnp.logical_not(jnp.logical_or(all_left, all_right)))
    def _():
        def body(c, carry):
            r = pl.multiple_of(c * rows, rows)
            bt = bt_ref[pl.ds(r, rows), :]
            acc = [jnp.full(bt.shape, -jnp.inf, F32) for _ in range(H)]
            for b in range(n_buckets):
                hit = bt == b
                for h in range(H):
                    acc[h] = jnp.where(hit, rb_ref[b, h] * LOG2E, acc[h])
            for h in range(H):
                o_ref[h, pl.ds(r, rows), :] = acc[h]
            return carry

        lax.fori_loop(0, tq // rows, body, 0)


def _bias_bank(rel_bias, plan):
    n_buckets, H = rel_bias.shape
    bt = _bucket_tiles(plan, n_buckets)
    nq, nt, tq, tk = bt.shape
    return pl.pallas_call(
        functools.partial(_bank_kernel, n_buckets=n_buckets, tq=tq, tk=tk, nk=plan.nk,
                          n_near=plan.n_near),
        grid=(nq, nt),
        in_specs=[pl.BlockSpec(memory_space=pltpu.SMEM),
                  pl.BlockSpec((None, None, tq, tk), lambda i, t: (i, t, 0, 0))],
        out_specs=pl.BlockSpec((H, None, None, tq, tk), lambda i, t: (0, i, t, 0, 0)),
        out_shape=jax.ShapeDtypeStruct((H, nq, nt, tq, tk), F32),
        compiler_params=_params(_ARB2),
        name="bias_bank",
    )(rel_bias, bt)


def _attn_kernel(rb_ref, lam0_ref, q_ref, k_ref, v_ref, bank_ref, lq1_ref, lk1_ref, lq2_ref,
                 lk2_ref, sg_ref, o_ref, m1, l1, a1, m2, l2, a2, kt, vt, *,
                 tq, tk, nk, n_tail, n_near, dh, n_buckets, row_tiles):
    h = pl.program_id(0)
    i = pl.program_id(1)
    lam_init = lam0_ref[0]
    out_gain = lam0_ref[1]
    V = v_ref.shape[-1]
    q0 = i * tq
    jl, jr = _near_range(q0, tq, tk, nk, jnp.minimum, jnp.maximum)
    c_left = rb_ref[n_buckets // 2 - 1, h] * LOG2E
    c_right = rb_ref[n_buckets - 1, h] * LOG2E

    for m_ref, l_ref, a_ref in ((m1, l1, a1), (m2, l2, a2)):
        m_ref[...] = jnp.full(m_ref.shape, -jnp.inf, F32)
        l_ref[...] = jnp.zeros(l_ref.shape, F32)
        a_ref[...] = jnp.zeros(a_ref.shape, F32)

    maps = ((slice(0, dh), m1, l1, a1), (slice(dh, 2 * dh), m2, l2, a2))

    def update(rows, kmap, vc, bias, qcols, m_ref, l_ref, a_ref):
        s = lax.dot_general(q_ref[rows, qcols], kmap, (((1,), (1,)), ((), ())),
                            preferred_element_type=F32)
        if bias is not None:
            s = s + bias
        lane_tiles = s.shape[1] // V7X_LANES
        m_prev = m_ref[rows, :]
        m_next = jnp.maximum(m_prev, jnp.max(s, axis=1, keepdims=True))
        p = jnp.exp2(s - jnp.tile(m_next, (1, lane_tiles)))
        alpha = jnp.exp2(m_prev - m_next)
        psum = p[:, 0:V7X_LANES]
        for t in range(1, lane_tiles):
            psum = psum + p[:, t * V7X_LANES:(t + 1) * V7X_LANES]
        l_ref[rows, :] = alpha * l_ref[rows, :] + psum
        m_ref[rows, :] = m_next
        pv = jnp.dot(p.astype(BF16), vc, preferred_element_type=F32)
        a_ref[rows, :] = a_ref[rows, :] * jnp.tile(alpha, (1, V // V7X_LANES)) + pv

    def chunk(kv_rows, width, bias_tile, k_src, v_src):
        for r0, rq in row_tiles:
            rows = slice(r0, r0 + rq)
            for qcols, m_ref, l_ref, a_ref in maps:
                bias = None if bias_tile is None else bank_ref[bias_tile, rows, 0:width]
                update(rows, k_src[kv_rows, qcols], v_src[kv_rows, :], bias,
                       qcols, m_ref, l_ref, a_ref)

    def far_chunk(j):
        chunk(pl.ds(pl.multiple_of(j * tk, tk), tk), tk, None, k_ref, v_ref)

    def near_chunk(j):
        chunk(pl.ds(pl.multiple_of(j * tk, tk), tk), tk, j - jl, k_ref, v_ref)

    def sweep(lo, hi, one, unrolls):
        for unroll in unrolls:
            def group(t, carry, lo=lo, unroll=unroll):
                for u in range(unroll):
                    one(lo + unroll * t + u)
                return carry

            n_groups = (hi - lo) // unroll
            lax.fori_loop(0, n_groups, group, 0)
            lo = lo + unroll * n_groups

    def shift_max(delta):
        m1[...] = m1[...] + delta
        m2[...] = m2[...] + delta

    far_unrolls = (_ATTN_FAR_UNROLL, _ATTN_UNROLL, 1)
    sweep(0, jl, far_chunk, far_unrolls)
    shift_max(c_left)
    sweep(jl, jr, near_chunk, (_ATTN_UNROLL, 1))
    shift_max(-c_right)
    sweep(jr, nk, far_chunk, far_unrolls)
    shift_max(c_right)

    if n_tail:
        kt[...] = jnp.zeros(kt.shape, BF16)
        vt[...] = jnp.zeros(vt.shape, BF16)
        kt[0:n_tail, :] = k_ref[nk * tk:nk * tk + n_tail, :]
        vt[0:n_tail, :] = v_ref[nk * tk:nk * tk + n_tail, :]
        chunk(slice(0, kt.shape[0]), kt.shape[0], n_near, kt, vt)

    lam = (jnp.exp(jnp.sum(lq1_ref[...] * lk1_ref[...], axis=-1, keepdims=True))
           - jnp.exp(jnp.sum(lq2_ref[...] * lk2_ref[...], axis=-1, keepdims=True))
           + lam_init)
    for r0, rq in row_tiles:
        rows = slice(r0, r0 + rq)
        w1 = 1.0 / jnp.sum(l1[rows, :], axis=1, keepdims=True)
        w2 = lam / jnp.sum(l2[rows, :], axis=1, keepdims=True)
        att = a1[rows, :] * w1 - a2[rows, :] * w2
        ms = jnp.mean(att * att, axis=-1, keepdims=True)
        y = att * lax.rsqrt(ms + EPS) * sg_ref[...] * out_gain
        o_ref[rows, :] = y.astype(BF16)


def _attention(proj3, bank, rel_bias, lq1, lk1, lq2, lk2, subln_g, plan, H, dh, V, lam_init):
    B, L, _ = proj3.shape
    assert V == 2 * dh and V % V7X_LANES == 0 and plan.tk % V7X_LANES == 0
    tq, tk = plan.tq, plan.tk
    nt = plan.n_near + 1
    tail_w = max(V7X_LANES, -(-plan.n_tail // V7X_LANES) * V7X_LANES)
    kern = functools.partial(
        _attn_kernel, tq=tq, tk=tk, nk=plan.nk, n_tail=plan.n_tail, n_near=plan.n_near,
        dh=dh, n_buckets=rel_bias.shape[0], row_tiles=plan.row_tiles)
    vec = lambda w: pl.BlockSpec((1, w), lambda h, i, b: (0, 0))
    lam0 = jnp.asarray([lam_init, 1.0 - lam_init], F32)
    return pl.pallas_call(
        kern,
        grid=(H, plan.nq, B),
        in_specs=[pl.BlockSpec(memory_space=pltpu.SMEM),
                  pl.BlockSpec(memory_space=pltpu.SMEM),
                  pl.BlockSpec((None, tq, V), lambda h, i, b: (b, i, h)),
                  pl.BlockSpec((None, L, V), lambda h, i, b: (b, 0, H + h)),
                  pl.BlockSpec((None, L, V), lambda h, i, b: (b, 0, 2 * H + h)),
                  pl.BlockSpec((None, None, nt, tq, tk), lambda h, i, b: (h, i, 0, 0, 0)),
                  vec(dh), vec(dh), vec(dh), vec(dh), vec(V)],
        out_specs=pl.BlockSpec((None, tq, V), lambda h, i, b: (b, i, h)),
        out_shape=jax.ShapeDtypeStruct((B, L, H * V), BF16),
        scratch_shapes=[pltpu.VMEM((tq, V7X_LANES), F32), pltpu.VMEM((tq, V7X_LANES), F32),
                        pltpu.VMEM((tq, V), F32),
                        pltpu.VMEM((tq, V7X_LANES), F32), pltpu.VMEM((tq, V7X_LANES), F32),
                        pltpu.VMEM((tq, V), F32),
                        pltpu.VMEM((tail_w, V), BF16), pltpu.VMEM((tail_w, V), BF16)],
        compiler_params=_params(_ARB3),
        name="attn",
    )(rel_bias, lam0, proj3, proj3, proj3, bank, lq1, lk1, lq2, lk2, subln_g)


_CONV_HALO = 16
_CONV_LANES = 256


def _conv_rows(tl):
    return _divisor_tile(tl, 48, V7X_BF16_SUBLANES)


def _conv_kernel(ua_ref, ub_ref, pa_ref, pb_ref, na_ref, nb_ref, w_ref, cb_ref, lg_ref, lb_ref,
                 o_ref, cpad, y_scr, *, tl, C, KW):
    i = pl.program_id(1)
    last = pl.num_programs(1) - 1
    H0 = _CONV_HALO
    RC = _conv_rows(tl)
    WIN = RC + 2 * H0
    off = H0 - KW // 2

    def glu(a, b):
        return a.astype(F32) * _sigmoid(b.astype(F32))

    cpad[0:H0, :] = jnp.where(i > 0, glu(pa_ref[...], pb_ref[...]), 0.0)
    cpad[H0 + tl:H0 + tl + H0, :] = jnp.where(i < last, glu(na_ref[...], nb_ref[...]), 0.0)

    def glu_body(c, carry):
        r = pl.multiple_of(c * RC, RC)
        cpad[pl.ds(H0 + r, RC), :] = glu(ua_ref[pl.ds(r, RC), :], ub_ref[pl.ds(r, RC), :])
        return carry

    lax.fori_loop(0, tl // RC, glu_body, 0)

    cg = min(C, _CONV_LANES)

    def conv_chunk(c):
        r = pl.multiple_of(c * RC, RC)
        for g in range(C // cg):
            cols = slice(g * cg, (g + 1) * cg)
            win = cpad[pl.ds(r, WIN), cols]
            acc = jnp.zeros((RC, cg), F32)
            for sh in range(8):
                taps = [t for t in range(KW) if (t + off) % 8 == sh]
                if not taps:
                    continue
                wsh = win if sh == 0 else pltpu.roll(win, WIN - sh, axis=0)
                for t in taps:
                    a = (t + off) // 8
                    acc = acc + wsh[8 * a:8 * a + RC, :] * w_ref[t:t + 1, cols]
            y_scr[pl.ds(r, RC), cols] = acc + cb_ref[:, cols]

    def ln_chunk(c):
        r = pl.multiple_of(c * RC, RC)
        x = y_scr[pl.ds(r, RC), :]
        mu = jnp.mean(x, axis=-1, keepdims=True)
        var = jnp.mean(jnp.square(x - mu), axis=-1, keepdims=True)
        y = (x - mu) * lax.rsqrt(var + LN_EPS) * lg_ref[...] + lb_ref[...]
        o_ref[pl.ds(r, RC), :] = (y * _sigmoid(y)).astype(BF16)

    conv_chunk(0)

    def body(c, carry):
        conv_chunk(c)
        ln_chunk(c - 1)
        return carry

    n_chunks = tl // RC
    lax.fori_loop(1, n_chunks, body, 0)
    ln_chunk(n_chunks - 1)


def _conv_module(proj3, conv_w, conv_b, ln_g, ln_b, u_start, tl):
    B, L, _ = proj3.shape
    KW, C = conv_w.shape
    H0 = _CONV_HALO
    assert KW // 2 <= H0 and KW - 1 + H0 - KW // 2 < 2 * H0 + 1
    assert u_start % C == 0 and L % tl == 0 and tl % V7X_BF16_SUBLANES == 0 and L % H0 == 0
    ca = u_start // C
    hb = tl // H0
    nhb = L // H0
    main = lambda col: pl.BlockSpec((None, tl, C), lambda b, i: (b, i, col))
    prev = lambda col: pl.BlockSpec((None, H0, C), lambda b, i: (b, jnp.maximum(i * hb - 1, 0), col))
    nxt = lambda col: pl.BlockSpec((None, H0, C), lambda b, i: (b, jnp.minimum((i + 1) * hb, nhb - 1), col))
    vec = pl.BlockSpec((1, C), lambda b, i: (0, 0))
    return pl.pallas_call(
        functools.partial(_conv_kernel, tl=tl, C=C, KW=KW),
        grid=(B, L // tl),
        in_specs=[main(ca), main(ca + 1), prev(ca), prev(ca + 1), nxt(ca), nxt(ca + 1),
                  pl.BlockSpec((KW, C), lambda b, i: (0, 0)), vec, vec, vec],
        out_specs=pl.BlockSpec((None, tl, C), lambda b, i: (b, i, 0)),
        out_shape=jax.ShapeDtypeStruct((B, L, C), BF16),
        scratch_shapes=[pltpu.VMEM((tl + 2 * H0, C), F32), pltpu.VMEM((tl, C), F32)],
        compiler_params=_params(_ARB2),
        name="conv",
    )(proj3, proj3, proj3, proj3, proj3, proj3, conv_w, conv_b, ln_g, ln_b)


def _merge_kernel(att_ref, cn_ref, wa_ref, wc_ref, ga_ref, gc_ref, o_ref):
    a = jnp.dot(att_ref[...], wa_ref[...], preferred_element_type=F32)
    c = jnp.dot(cn_ref[...], wc_ref[...], preferred_element_type=F32)
    o_ref[...] = (ga_ref[...].astype(F32) * a + gc_ref[...].astype(F32) * c).astype(BF16)


def _merge(att, cn, wa, wc, layer, proj, gate_start, tm, tn):
    M, A = att.shape
    C = cn.shape[1]
    D = wa.shape[2]
    assert gate_start % tn == 0 and D % tn == 0
    ga0 = gate_start // tn
    gc0 = (gate_start + D) // tn
    return pl.pallas_call(
        _merge_kernel,
        grid=(M // tm, D // tn),
        in_specs=[pl.BlockSpec((tm, A), lambda m, n: (m, 0)),
                  pl.BlockSpec((tm, C), lambda m, n: (m, 0)),
                  pl.BlockSpec((None, A, tn), lambda m, n: (layer, 0, n)),
                  pl.BlockSpec((None, C, tn), lambda m, n: (layer, 0, n)),
                  pl.BlockSpec((tm, tn), lambda m, n: (m, ga0 + n)),
                  pl.BlockSpec((tm, tn), lambda m, n: (m, gc0 + n))],
        out_specs=pl.BlockSpec((tm, tn), lambda m, n: (m, n)),
        out_shape=jax.ShapeDtypeStruct((M, D), BF16),
        compiler_params=_params(_ARB2),
        name="merge",
    )(att, cn, wa, wc, proj, proj)


def _final_norm_kernel(h_ref, g_ref, o_ref):
    x = h_ref[...]
    ms = jnp.mean(x * x, axis=-1, keepdims=True)
    o_ref[...] = x * lax.rsqrt(ms + EPS) * g_ref[...]


def _final_norm(h, g, B, L, n_meta, tr):
    D = h.shape[1]
    S = L - n_meta
    assert L % 8 == 0 and n_meta % 8 == 0 and tr % 8 == 0
    return pl.pallas_call(
        _final_norm_kernel,
        grid=(B, S // tr),
        in_specs=[pl.BlockSpec((pl.Element(tr), pl.Element(D)),
                               lambda b, i: (pl.multiple_of(b * L + n_meta + i * tr, 8), 0)),
                  pl.BlockSpec((1, D), lambda b, i: (0, 0))],
        out_specs=pl.BlockSpec((None, tr, D), lambda b, i: (b, i, 0)),
        out_shape=jax.ShapeDtypeStruct((B, S, D), F32),
        compiler_params=_params(_ARB2),
        name="final_norm",
    )(h, g)


def _conv_tile(L):
    for mult in (48, 32, V7X_BF16_SUBLANES):
        if L % mult == 0:
            return _divisor_tile(L, 2100, mult)
    raise ValueError(f"sequence length {L} is not a multiple of {V7X_BF16_SUBLANES}")


def _tiles(M, L, S, D, F, q_width, gate_start, d_in):
    sub = V7X_BF16_SUBLANES
    return dict(
        tm_norm=_divisor_tile(M, 704, sub),
        tm=_divisor_tile(M, 1024, sub),
        tf=_divisor_tile(F, 512, V7X_LANES),
        tn_res=_divisor_tile(D, 1024, V7X_LANES),
        tm_down=_divisor_tile(M, 2100, sub),
        tn_down=_divisor_tile(D, 512, V7X_LANES),
        tn_proj=_divisor_tile(math.gcd(q_width, gate_start, d_in), 1024, V7X_LANES),
        tn_merge=_divisor_tile(math.gcd(D, gate_start), 1024, V7X_LANES),
        tl=_conv_tile(L),
        tr=_divisor_tile(S, 512, 8),
        tq_target=704 if L <= 4096 else 448,
        tk=256,
    )


def _encode(x, meta, rel_bias, wts, final_norm, dims):
    H, dh, V = dims["H"], dims["dh"], dims["V"]
    B, S, D = x.shape
    n_meta = meta.shape[0]
    L = S + n_meta
    M = B * L
    depth = wts["w_in"].shape[0]
    F = wts["w_ffn1_gate"].shape[-1]
    KW, C = wts["conv_w"].shape[1:]
    d_in = wts["w_in"].shape[-1]
    qk_width = H * 2 * dh
    u_start = 2 * qk_width + H * V
    gate_start = u_start + 2 * C
    assert d_in == gate_start + 2 * D

    t = _tiles(M, L, S, D, F, qk_width, gate_start, d_in)
    plan = _AttnPlan(L, t["tq_target"], t["tk"])
    bank = _bias_bank(rel_bias, plan)
    qscale = dh ** -0.5 * LOG2E

    h = jnp.concatenate([jnp.broadcast_to(meta[None].astype(x.dtype), (B, n_meta, D)), x],
                        axis=1).reshape(M, D)
    ssq = _row_ssq(h, t["tm_norm"])
    row = lambda v: v.reshape(1, -1)
    for l in range(depth):
        lam_init = 0.8 - 0.6 * math.exp(-0.3 * l)
        hm = _ffn_up(h, ssq, row(wts["norm_ffn1"][l]), wts["w_ffn1_gate"], wts["w_ffn1_up"], l,
                     t["tm_norm"], t["tf"])
        h, ssq = _mm_res(hm, wts["w_ffn1_down"], l, h, 0.5, t["tm_down"], t["tn_down"])

        proj = _proj(h, ssq, row(wts["norm_mix"][l]), wts["w_in"], l, t["tm_norm"], t["tn_proj"],
                     qk_width, gate_start, qscale)
        proj3 = proj.reshape(B, L, d_in)
        att = _attention(proj3, bank, rel_bias, row(wts["lambda_q1"][l]), row(wts["lambda_k1"][l]),
                         row(wts["lambda_q2"][l]), row(wts["lambda_k2"][l]), row(wts["subln_g"][l]),
                         plan, H, dh, V, lam_init)
        cn = _conv_module(proj3, wts["conv_w"][l], row(wts["conv_b"][l]), row(wts["conv_ln_g"][l]),
                          row(wts["conv_ln_b"][l]), u_start, t["tl"])
        mg = _merge(att.reshape(M, H * V), cn.reshape(M, C), wts["w_attn_out"],
                    wts["w_conv_out"], l, proj, gate_start, t["tm"], t["tn_merge"])
        h, ssq = _mm_res(mg, wts["w_out"], l, h, 1.0, t["tm"], t["tn_res"])

        hm = _ffn_up(h, ssq, row(wts["norm_ffn2"][l]), wts["w_ffn2_gate"], wts["w_ffn2_up"], l,
                     t["tm_norm"], t["tf"])
        h, ssq = _mm_res(hm, wts["w_ffn2_down"], l, h, 0.5, t["tm_down"], t["tn_down"])
    return _final_norm(h, row(final_norm), B, L, n_meta, t["tr"])


_MATMUL_WEIGHTS = ("w_ffn1_gate", "w_ffn1_up", "w_ffn1_down", "w_in", "w_attn_out",
                   "w_conv_out", "w_out", "w_ffn2_gate", "w_ffn2_up", "w_ffn2_down")


def kernel(x_prompt, x_sample, meta_tokens, rel_bias, norm_ffn1, w_ffn1_gate, w_ffn1_up, w_ffn1_down, norm_mix, w_in, lambda_q1, lambda_k1, lambda_q2, lambda_k2, subln_g, w_attn_out, conv_w, conv_b, conv_ln_g, conv_ln_b, w_conv_out, w_out, norm_ffn2, w_ffn2_gate, w_ffn2_up, w_ffn2_down, final_norm):
    wts = dict(norm_ffn1=norm_ffn1, w_ffn1_gate=w_ffn1_gate, w_ffn1_up=w_ffn1_up,
               w_ffn1_down=w_ffn1_down, norm_mix=norm_mix, w_in=w_in, lambda_q1=lambda_q1,
               lambda_k1=lambda_k1, lambda_q2=lambda_q2, lambda_k2=lambda_k2, subln_g=subln_g,
               w_attn_out=w_attn_out, conv_w=conv_w, conv_b=conv_b, conv_ln_g=conv_ln_g,
               conv_ln_b=conv_ln_b, w_conv_out=w_conv_out, w_out=w_out, norm_ffn2=norm_ffn2,
               w_ffn2_gate=w_ffn2_gate, w_ffn2_up=w_ffn2_up, w_ffn2_down=w_ffn2_down)
    for name in _MATMUL_WEIGHTS:
        wts[name] = wts[name].astype(BF16)
    dh = lambda_q1.shape[-1]
    V = subln_g.shape[-1]
    dims = dict(H=rel_bias.shape[1], dh=dh, V=V)
    y_prompt = _encode(x_prompt, meta_tokens, rel_bias, wts, final_norm, dims)
    y_sample = _encode(x_sample, meta_tokens, rel_bias, wts, final_norm, dims)
    return (y_prompt, y_sample)
```

```python
import functools
import math

import jax
import jax.numpy as jnp
from jax import lax
from jax.experimental import pallas as pl
from jax.experimental.pallas import tpu as pltpu

F32 = jnp.float32
BF16 = jnp.bfloat16

EPS = 1e-6
LN_EPS = 1e-5
MAX_DISTANCE = 128
LOG2E = math.log2(math.e)

V7X_LANES = 128
V7X_BF16_SUBLANES = 16
V7X_VMEM_LIMIT_BYTES = 56 * 1024 * 1024
V7X_VMEM_LOOKAHEAD_BYTES = 52 * 1024 * 1024

_ARB2 = ("arbitrary", "arbitrary")
_ARB3 = ("arbitrary", "arbitrary", "arbitrary")


def _params(sem):
    return pltpu.CompilerParams(dimension_semantics=sem,
                                vmem_limit_bytes=V7X_VMEM_LIMIT_BYTES)


def _sigmoid(x):
    return 0.5 * jnp.tanh(0.5 * x) + 0.5


def _divisor_tile(n, target, mult):
    best = None
    for t in range(mult, min(n, target) + 1, mult):
        if n % t == 0:
            best = t
    if best is None:
        raise ValueError(f"no tile for n={n} target={target} mult={mult}")
    return best


_NORM_UNROLL = 4


def _lookahead_rows(n_row_blocks):
    return lambda m, n: (jnp.minimum(m + jnp.minimum(n, 1), n_row_blocks - 1), 0)


def _norm_chunk(h_ref, ssq_ref, g, xn_ref, chunk):
    rows = V7X_BF16_SUBLANES
    r = pl.multiple_of(chunk * rows, rows)
    x = h_ref[pl.ds(r, rows), :]
    ms = jnp.sum(ssq_ref[pl.ds(r, rows), :], axis=1, keepdims=True) * (1.0 / x.shape[1])
    xn_ref[pl.ds(r, rows), :] = (x * lax.rsqrt(ms + EPS) * g).astype(BF16)


def _norm_block(h_ref, ssq_ref, g_ref, xn_ref):
    g = g_ref[...]

    def body(i, c):
        _norm_chunk(h_ref, ssq_ref, g, xn_ref, i)
        return c

    lax.fori_loop(0, h_ref.shape[0] // V7X_BF16_SUBLANES, body, 0, unroll=_NORM_UNROLL)


def _norm_share(h_ref, ssq_ref, g_ref, xn_ref, step, n_steps):
    n_chunks = h_ref.shape[0] // V7X_BF16_SUBLANES
    per_step = -(-n_chunks // n_steps)
    g = g_ref[...]
    for u in range(per_step):
        _norm_chunk(h_ref, ssq_ref, g, xn_ref,
                    jnp.minimum(step * per_step + u, n_chunks - 1))


def _normed_matmul_steps(h_ref, ssq_ref, g_ref, xn_bufs, step_body):
    m = pl.program_id(0)
    n = pl.program_id(1)
    n_steps = pl.num_programs(1) - 1

    if len(xn_bufs) == 1:
        @pl.when(n == 0)
        def _():
            _norm_block(h_ref, ssq_ref, g_ref, xn_bufs[0])

        step_body(xn_bufs[0])
        return

    @pl.when(jnp.logical_and(m == 0, n == 0))
    def _():
        _norm_block(h_ref, ssq_ref, g_ref, xn_bufs[0])
        step_body(xn_bufs[0])

    for parity in (0, 1):
        mine, other = xn_bufs[parity], xn_bufs[1 - parity]

        @pl.when(jnp.logical_and(jnp.logical_and(m > 0, n == 0), m % 2 == parity))
        def _(mine=mine):
            step_body(mine)

        @pl.when(jnp.logical_and(n > 0, m % 2 == parity))
        def _(mine=mine, other=other):
            _norm_share(h_ref, ssq_ref, g_ref, other, n - 1, n_steps)
            step_body(mine)


def _row_ssq_kernel(h_ref, o_ref):
    x = h_ref[...]
    sq = x * x
    part = sq[:, 0:V7X_LANES]
    for t in range(1, x.shape[1] // V7X_LANES):
        part = part + sq[:, t * V7X_LANES:(t + 1) * V7X_LANES]
    o_ref[...] = part


def _row_ssq(h, tr):
    M, D = h.shape
    return pl.pallas_call(
        _row_ssq_kernel,
        grid=(M // tr,),
        in_specs=[pl.BlockSpec((tr, D), lambda m: (m, 0))],
        out_specs=pl.BlockSpec((tr, V7X_LANES), lambda m: (m, 0)),
        out_shape=jax.ShapeDtypeStruct((M, V7X_LANES), F32),
        compiler_params=_params(("arbitrary",)),
        name="row_ssq",
    )(h)


def _ffn_up_kernel(h_ref, ssq_ref, g_ref, wg_ref, wu_ref, o_ref, *xn_bufs):
    def step(xn_ref):
        xn = xn_ref[...]
        a = jnp.dot(xn, wg_ref[...], preferred_element_type=F32)
        b = jnp.dot(xn, wu_ref[...], preferred_element_type=F32)
        o_ref[...] = (a * _sigmoid(a) * b).astype(BF16)

    _normed_matmul_steps(h_ref, ssq_ref, g_ref, xn_bufs, step)


def _ffn_up(h, ssq, g, wg, wu, layer, tm, tf):
    M, D = h.shape
    F = wg.shape[2]
    fixed = 2 * tm * D * 4 + 2 * 2 * D * tf * 2 + 2 * tm * tf * 2 + 4 * tm * tf * 4
    lookahead = F // tf >= 2 and fixed + 2 * tm * D * 2 <= V7X_VMEM_LOOKAHEAD_BYTES
    rows = _lookahead_rows(M // tm) if lookahead else (lambda m, f: (m, 0))
    return pl.pallas_call(
        _ffn_up_kernel,
        grid=(M // tm, F // tf),
        in_specs=[pl.BlockSpec((tm, D), rows),
                  pl.BlockSpec((tm, V7X_LANES), rows),
                  pl.BlockSpec((1, D), lambda m, f: (0, 0)),
                  pl.BlockSpec((None, D, tf), lambda m, f: (layer, 0, f)),
                  pl.BlockSpec((None, D, tf), lambda m, f: (layer, 0, f))],
        out_specs=pl.BlockSpec((tm, tf), lambda m, f: (m, f)),
        out_shape=jax.ShapeDtypeStruct((M, F), BF16),
        scratch_shapes=[pltpu.VMEM((tm, D), BF16)] * (2 if lookahead else 1),
        compiler_params=_params(_ARB2),
        name="ffn_up",
    )(h, ssq, g, wg, wu)


def _mm_res_kernel(x_ref, w_ref, h_ref, o_ref, ssq_ref, *, scale):
    @pl.when(pl.program_id(1) == 0)
    def _():
        ssq_ref[...] = jnp.zeros(ssq_ref.shape, F32)

    acc = jnp.dot(x_ref[...], w_ref[...], preferred_element_type=F32)
    y = h_ref[...] + scale * acc
    o_ref[...] = y
    sq = y * y
    part = sq[:, 0:V7X_LANES]
    for t in range(1, y.shape[1] // V7X_LANES):
        part = part + sq[:, t * V7X_LANES:(t + 1) * V7X_LANES]
    ssq_ref[...] = ssq_ref[...] + part


def _mm_res(x, w, layer, h, scale, tm, tn):
    M, K = x.shape
    N = w.shape[2]
    return pl.pallas_call(
        functools.partial(_mm_res_kernel, scale=scale),
        grid=(M // tm, N // tn),
        in_specs=[pl.BlockSpec((tm, K), lambda m, n: (m, 0)),
                  pl.BlockSpec((None, K, tn), lambda m, n: (layer, 0, n)),
                  pl.BlockSpec((tm, tn), lambda m, n: (m, n))],
        out_specs=[pl.BlockSpec((tm, tn), lambda m, n: (m, n)),
                   pl.BlockSpec((tm, V7X_LANES), lambda m, n: (m, 0))],
        out_shape=[jax.ShapeDtypeStruct((M, N), F32),
                   jax.ShapeDtypeStruct((M, V7X_LANES), F32)],
        compiler_params=_params(_ARB2),
        name="mm_res",
    )(x, w, h)


def _proj_kernel(h_ref, ssq_ref, g_ref, w_ref, o_ref, xn_ref, *, n_q, n_lin, qscale):
    n = pl.program_id(1)

    @pl.when(n == 0)
    def _():
        _norm_block(h_ref, ssq_ref, g_ref, xn_ref)

    def acc():
        return jnp.dot(xn_ref[...], w_ref[...], preferred_element_type=F32)

    @pl.when(n < n_q)
    def _():
        o_ref[...] = (acc() * qscale).astype(BF16)

    @pl.when(jnp.logical_and(n >= n_q, n < n_lin))
    def _():
        o_ref[...] = acc().astype(BF16)

    @pl.when(n >= n_lin)
    def _():
        o_ref[...] = _sigmoid(acc()).astype(BF16)


def _proj(h, ssq, g, w, layer, tm, tn, q_width, gate_start, qscale):
    M, D = h.shape
    N = w.shape[2]
    assert q_width % tn == 0 and gate_start % tn == 0
    return pl.pallas_call(
        functools.partial(_proj_kernel, n_q=q_width // tn, n_lin=gate_start // tn,
                          qscale=qscale),
        grid=(M // tm, N // tn),
        in_specs=[pl.BlockSpec((tm, D), lambda m, n: (m, 0)),
                  pl.BlockSpec((tm, V7X_LANES), lambda m, n: (m, 0)),
                  pl.BlockSpec((1, D), lambda m, n: (0, 0)),
                  pl.BlockSpec((None, D, tn), lambda m, n: (layer, 0, n))],
        out_specs=pl.BlockSpec((tm, tn), lambda m, n: (m, n)),
        out_shape=jax.ShapeDtypeStruct((M, N), BF16),
        scratch_shapes=[pltpu.VMEM((tm, D), BF16)],
        compiler_params=_params(_ARB2),
        name="proj",
    )(h, ssq, g, w)


def _t5_bucket(rel, n_buckets):
    nb = n_buckets // 2
    max_exact = nb // 2
    ret = (rel > 0).astype(jnp.int32) * nb
    n = jnp.abs(rel)
    nf = jnp.maximum(n, 1).astype(jnp.float32)
    large = max_exact + (jnp.log(nf / max_exact) / math.log(MAX_DISTANCE / max_exact)
                         * (nb - max_exact)).astype(jnp.int32)
    large = jnp.minimum(large, nb - 1)
    return ret + jnp.where(n < max_exact, n, large)


_ATTN_TILE_ROWS = 216
_ATTN_UNROLL = 4
_ATTN_FAR_UNROLL = 8


class _AttnPlan:
    def __init__(self, L, tq_target, tk):
        self.L = L
        self.tk = tk
        self.tq = _divisor_tile(L, tq_target, V7X_BF16_SUBLANES)
        self.nq = L // self.tq
        self.nk = L // tk
        self.n_tail = L - self.nk * tk
        assert self.n_tail % V7X_BF16_SUBLANES == 0
        self.jl = [self.far_left(i * self.tq) for i in range(self.nq)]
        self.jr = [self.far_right(i * self.tq) for i in range(self.nq)]
        self.n_near = max(1, max(r - l for l, r in zip(self.jl, self.jr)))
        units = self.tq // V7X_BF16_SUBLANES
        n = max(1, round(self.tq / _ATTN_TILE_ROWS))
        sizes = [(units // n + (1 if t < units % n else 0)) * V7X_BF16_SUBLANES for t in range(n)]
        self.row_tiles = tuple((sum(sizes[:t]), sizes[t]) for t in range(n))

    def far_left(self, q0):
        return _near_range(q0, self.tq, self.tk, self.nk, min, max)[0]

    def far_right(self, q0):
        return _near_range(q0, self.tq, self.tk, self.nk, min, max)[1]


def _near_range(q0, tq, tk, nk, minimum, maximum):
    u = _ATTN_UNROLL
    lo = maximum(q0 - MAX_DISTANCE + 1, 0) // tk // u * u
    hi = (q0 + tq - 1 + MAX_DISTANCE + tk - 1) // tk
    hi = minimum((hi + u - 1) // u * u, nk)
    return minimum(lo, nk), hi


def _bias_vec_kernel(rb_ref, bk_ref, o_ref, *, n_buckets):
    bk = bk_ref[...]
    for h in range(o_ref.shape[0]):
        acc = jnp.zeros(bk.shape, F32)
        for b in range(n_buckets):
            acc = jnp.where(bk == b, rb_ref[b, h] * LOG2E, acc)
        o_ref[h:h + 1, :] = acc


def _bias_vectors(rel_bias, L, length):
    n_buckets, H = rel_bias.shape
    rel = jnp.arange(length, dtype=jnp.int32)[None, :] - (L - 1)
    return pl.pallas_call(
        functools.partial(_bias_vec_kernel, n_buckets=n_buckets),
        in_specs=[pl.BlockSpec(memory_space=pltpu.SMEM),
                  pl.BlockSpec((1, length), lambda: (0, 0))],
        out_specs=pl.BlockSpec((H, length), lambda: (0, 0)),
        out_shape=jax.ShapeDtypeStruct((H, length), F32),
        name="bias_vectors",
    )(rel_bias, _t5_bucket(rel, n_buckets))


_BANK_ROWS = 8
_BANK_UNROLL = 6


def _bank_kernel(rb_ref, bv_ref, o_ref, *, n_buckets, tq, tk, nk, n_near, n_tail, L):
    i = pl.program_id(0)
    t = pl.program_id(1)
    H = o_ref.shape[0]
    q0 = i * tq
    lo, _ = _near_range(q0, tq, tk, nk, jnp.minimum, jnp.maximum)
    is_chunk = t < n_near
    kstart = jnp.where(is_chunk, (lo + t) * tk, nk * tk)
    n_valid = jnp.where(is_chunk, tk, n_tail)
    all_left = jnp.logical_and(is_chunk, kstart + tk - 1 - q0 <= -MAX_DISTANCE)
    all_right = jnp.logical_and(is_chunk, kstart - (q0 + tq - 1) >= MAX_DISTANCE)

    def fill(bucket):
        for h in range(H):
            o_ref[h] = jnp.full((tq, tk), rb_ref[bucket, h] * LOG2E, F32)

    @pl.when(all_left)
    def _():
        fill(n_buckets // 2 - 1)

    @pl.when(all_right)
    def _():
        fill(n_buckets - 1)

    @pl.when(jnp.logical_not(jnp.logical_or(all_left, all_right)))
    def _():
        R = _BANK_ROWS
        win_w = tk + 2 * V7X_LANES
        col = lax.broadcasted_iota(jnp.int32, (R, tk), 1)

        def body(g, carry):
            r0 = g * R
            off = kstart - (q0 + r0 + R - 1) + (L - 1)
            off_al = pl.multiple_of(off // V7X_LANES * V7X_LANES, V7X_LANES)
            shift = (win_w - (R - 1) - (off - off_al)) % win_w
            for h in range(H):
                win = jnp.broadcast_to(bv_ref[h:h + 1, pl.ds(off_al, win_w)], (R, win_w))
                tile = pltpu.roll(win, shift, 1, stride=1, stride_axis=0)[:, :tk]
                o_ref[h, pl.ds(pl.multiple_of(r0, R), R), :] = jnp.where(col < n_valid, tile,
                                                                          -jnp.inf)
            return carry

        lax.fori_loop(0, tq // R, body, 0, unroll=_BANK_UNROLL)


def _bias_bank(rel_bias, plan):
    n_buckets, H = rel_bias.shape
    tq, tk, nq, nt = plan.tq, plan.tk, plan.nq, plan.n_near + 1
    length = -(-(plan.L + (plan.nk + nt) * tk + tk + 2 * V7X_LANES) // V7X_LANES) * V7X_LANES
    bvec = _bias_vectors(rel_bias, plan.L, length)
    return pl.pallas_call(
        functools.partial(_bank_kernel, n_buckets=n_buckets, tq=tq, tk=tk, nk=plan.nk,
                          n_near=plan.n_near, n_tail=plan.n_tail, L=plan.L),
        grid=(nq, nt),
        in_specs=[pl.BlockSpec(memory_space=pltpu.SMEM),
                  pl.BlockSpec((H, length), lambda i, t: (0, 0))],
        out_specs=pl.BlockSpec((H, None, None, tq, tk), lambda i, t: (0, i, t, 0, 0)),
        out_shape=jax.ShapeDtypeStruct((H, nq, nt, tq, tk), F32),
        compiler_params=_params(_ARB2),
        name="bias_bank",
    )(rel_bias, bvec)


def _attn_kernel(rb_ref, lam0_ref, q_ref, k_ref, v_ref, bank_ref, lq1_ref, lk1_ref, lq2_ref,
                 lk2_ref, sg_ref, o_ref, m1, l1, a1, m2, l2, a2, kt, vt, *,
                 tq, tk, nk, n_tail, n_near, dh, n_buckets, row_tiles):
    h = pl.program_id(0)
    i = pl.program_id(1)
    lam_init = lam0_ref[0]
    out_gain = lam0_ref[1]
    V = v_ref.shape[-1]
    q0 = i * tq
    jl, jr = _near_range(q0, tq, tk, nk, jnp.minimum, jnp.maximum)
    c_left = rb_ref[n_buckets // 2 - 1, h] * LOG2E
    c_right = rb_ref[n_buckets - 1, h] * LOG2E

    for m_ref, l_ref, a_ref in ((m1, l1, a1), (m2, l2, a2)):
        m_ref[...] = jnp.full(m_ref.shape, -jnp.inf, F32)
        l_ref[...] = jnp.zeros(l_ref.shape, F32)
        a_ref[...] = jnp.zeros(a_ref.shape, F32)

    maps = ((slice(0, dh), m1, l1, a1), (slice(dh, 2 * dh), m2, l2, a2))

    def update(rows, kmap, vc, bias, qcols, m_ref, l_ref, a_ref):
        s = lax.dot_general(q_ref[rows, qcols], kmap, (((1,), (1,)), ((), ())),
                            preferred_element_type=F32)
        if bias is not None:
            s = s + bias
        lane_tiles = s.shape[1] // V7X_LANES
        m_prev = m_ref[rows, :]
        m_next = jnp.maximum(m_prev, jnp.max(s, axis=1, keepdims=True))
        p = jnp.exp2(s - jnp.tile(m_next, (1, lane_tiles)))
        alpha = jnp.exp2(m_prev - m_next)
        psum = p[:, 0:V7X_LANES]
        for t in range(1, lane_tiles):
            psum = psum + p[:, t * V7X_LANES:(t + 1) * V7X_LANES]
        l_ref[rows, :] = alpha * l_ref[rows, :] + psum
        m_ref[rows, :] = m_next
        pv = jnp.dot(p.astype(BF16), vc, preferred_element_type=F32)
        a_ref[rows, :] = a_ref[rows, :] * jnp.tile(alpha, (1, V // V7X_LANES)) + pv

    def chunk(kv_rows, width, bias_tile, k_src, v_src):
        for r0, rq in row_tiles:
            rows = slice(r0, r0 + rq)
            for qcols, m_ref, l_ref, a_ref in maps:
                bias = None if bias_tile is None else bank_ref[bias_tile, rows, 0:width]
                update(rows, k_src[kv_rows, qcols], v_src[kv_rows, :], bias,
                       qcols, m_ref, l_ref, a_ref)

    def far_chunk(j):
        chunk(pl.ds(pl.multiple_of(j * tk, tk), tk), tk, None, k_ref, v_ref)

    def near_chunk(j):
        chunk(pl.ds(pl.multiple_of(j * tk, tk), tk), tk, j - jl, k_ref, v_ref)

    def sweep(lo, hi, one, unrolls):
        for unroll in unrolls:
            def group(t, carry, lo=lo, unroll=unroll):
                for u in range(unroll):
                    one(lo + unroll * t + u)
                return carry

            n_groups = (hi - lo) // unroll
            lax.fori_loop(0, n_groups, group, 0)
            lo = lo + unroll * n_groups

    def shift_max(delta):
        m1[...] = m1[...] + delta
        m2[...] = m2[...] + delta

    far_unrolls = (_ATTN_FAR_UNROLL, _ATTN_UNROLL, 1)
    sweep(0, jl, far_chunk, far_unrolls)
    shift_max(c_left)
    sweep(jl, jr, near_chunk, (_ATTN_UNROLL, 1))
    shift_max(-c_right)
    sweep(jr, nk, far_chunk, far_unrolls)
    shift_max(c_right)

    if n_tail:
        kt[...] = jnp.zeros(kt.shape, BF16)
        vt[...] = jnp.zeros(vt.shape, BF16)
        kt[0:n_tail, :] = k_ref[nk * tk:nk * tk + n_tail, :]
        vt[0:n_tail, :] = v_ref[nk * tk:nk * tk + n_tail, :]
        chunk(slice(0, kt.shape[0]), kt.shape[0], n_near, kt, vt)

    lam = (jnp.exp(jnp.sum(lq1_ref[...] * lk1_ref[...], axis=-1, keepdims=True))
           - jnp.exp(jnp.sum(lq2_ref[...] * lk2_ref[...], axis=-1, keepdims=True))
           + lam_init)
    for r0, rq in row_tiles:
        rows = slice(r0, r0 + rq)
        w1 = 1.0 / jnp.sum(l1[rows, :], axis=1, keepdims=True)
        w2 = lam / jnp.sum(l2[rows, :], axis=1, keepdims=True)
        att = a1[rows, :] * w1 - a2[rows, :] * w2
        ms = jnp.mean(att * att, axis=-1, keepdims=True)
        y = att * lax.rsqrt(ms + EPS) * sg_ref[...] * out_gain
        o_ref[rows, :] = y.astype(BF16)


def _attention(proj3, bank, rel_bias, lq1, lk1, lq2, lk2, subln_g, plan, H, dh, V, lam_init):
    B, L, _ = proj3.shape
    assert V == 2 * dh and V % V7X_LANES == 0 and plan.tk % V7X_LANES == 0
    tq, tk = plan.tq, plan.tk
    nt = plan.n_near + 1
    tail_w = max(V7X_LANES, -(-plan.n_tail // V7X_LANES) * V7X_LANES)
    kern = functools.partial(
        _attn_kernel, tq=tq, tk=tk, nk=plan.nk, n_tail=plan.n_tail, n_near=plan.n_near,
        dh=dh, n_buckets=rel_bias.shape[0], row_tiles=plan.row_tiles)
    vec = lambda w: pl.BlockSpec((1, w), lambda h, i, b: (0, 0))
    lam0 = jnp.asarray([lam_init, 1.0 - lam_init], F32)
    return pl.pallas_call(
        kern,
        grid=(H, plan.nq, B),
        in_specs=[pl.BlockSpec(memory_space=pltpu.SMEM),
                  pl.BlockSpec(memory_space=pltpu.SMEM),
                  pl.BlockSpec((None, tq, V), lambda h, i, b: (b, i, h)),
                  pl.BlockSpec((None, L, V), lambda h, i, b: (b, 0, H + h)),
                  pl.BlockSpec((None, L, V), lambda h, i, b: (b, 0, 2 * H + h)),
                  pl.BlockSpec((None, None, nt, tq, tk), lambda h, i, b: (h, i, 0, 0, 0)),
                  vec(dh), vec(dh), vec(dh), vec(dh), vec(V)],
        out_specs=pl.BlockSpec((None, tq, V), lambda h, i, b: (b, i, h)),
        out_shape=jax.ShapeDtypeStruct((B, L, H * V), BF16),
        scratch_shapes=[pltpu.VMEM((tq, V7X_LANES), F32), pltpu.VMEM((tq, V7X_LANES), F32),
                        pltpu.VMEM((tq, V), F32),
                        pltpu.VMEM((tq, V7X_LANES), F32), pltpu.VMEM((tq, V7X_LANES), F32),
                        pltpu.VMEM((tq, V), F32),
                        pltpu.VMEM((tail_w, V), BF16), pltpu.VMEM((tail_w, V), BF16)],
        compiler_params=_params(_ARB3),
        name="attn",
    )(rel_bias, lam0, proj3, proj3, proj3, bank, lq1, lk1, lq2, lk2, subln_g)


_CONV_HALO = 16
_CONV_LANES = 256


def _conv_rows(tl):
    return _divisor_tile(tl, 48, V7X_BF16_SUBLANES)


def _conv_kernel(ua_ref, ub_ref, pa_ref, pb_ref, na_ref, nb_ref, w_ref, cb_ref, lg_ref, lb_ref,
                 o_ref, cpad, y_scr, *, tl, C, KW):
    i = pl.program_id(1)
    last = pl.num_programs(1) - 1
    H0 = _CONV_HALO
    RC = _conv_rows(tl)
    WIN = RC + 2 * H0
    off = H0 - KW // 2

    def glu(a, b):
        return a.astype(F32) * _sigmoid(b.astype(F32))

    cpad[0:H0, :] = jnp.where(i > 0, glu(pa_ref[...], pb_ref[...]), 0.0)
    cpad[H0 + tl:H0 + tl + H0, :] = jnp.where(i < last, glu(na_ref[...], nb_ref[...]), 0.0)

    def glu_body(c, carry):
        r = pl.multiple_of(c * RC, RC)
        cpad[pl.ds(H0 + r, RC), :] = glu(ua_ref[pl.ds(r, RC), :], ub_ref[pl.ds(r, RC), :])
        return carry

    lax.fori_loop(0, tl // RC, glu_body, 0)

    cg = min(C, _CONV_LANES)

    def conv_chunk(c):
        r = pl.multiple_of(c * RC, RC)
        for g in range(C // cg):
            cols = slice(g * cg, (g + 1) * cg)
            win = cpad[pl.ds(r, WIN), cols]
            acc = jnp.zeros((RC, cg), F32)
            for sh in range(8):
                taps = [t for t in range(KW) if (t + off) % 8 == sh]
                if not taps:
                    continue
                wsh = win if sh == 0 else pltpu.roll(win, WIN - sh, axis=0)
                for t in taps:
                    a = (t + off) // 8
                    acc = acc + wsh[8 * a:8 * a + RC, :] * w_ref[t:t + 1, cols]
            y_scr[pl.ds(r, RC), cols] = acc + cb_ref[:, cols]

    def ln_chunk(c):
        r = pl.multiple_of(c * RC, RC)
        x = y_scr[pl.ds(r, RC), :]
        mu = jnp.mean(x, axis=-1, keepdims=True)
        var = jnp.mean(jnp.square(x - mu), axis=-1, keepdims=True)
        y = (x - mu) * lax.rsqrt(var + LN_EPS) * lg_ref[...] + lb_ref[...]
        o_ref[pl.ds(r, RC), :] = (y * _sigmoid(y)).astype(BF16)

    conv_chunk(0)

    def body(c, carry):
        conv_chunk(c)
        ln_chunk(c - 1)
        return carry

    n_chunks = tl // RC
    lax.fori_loop(1, n_chunks, body, 0)
    ln_chunk(n_chunks - 1)


def _conv_module(proj3, conv_w, conv_b, ln_g, ln_b, u_start, tl):
    B, L, _ = proj3.shape
    KW, C = conv_w.shape
    H0 = _CONV_HALO
    assert KW // 2 <= H0 and KW - 1 + H0 - KW // 2 < 2 * H0 + 1
    assert u_start % C == 0 and L % tl == 0 and tl % V7X_BF16_SUBLANES == 0 and L % H0 == 0
    ca = u_start // C
    hb = tl // H0
    nhb = L // H0
    main = lambda col: pl.BlockSpec((None, tl, C), lambda b, i: (b, i, col))
    prev = lambda col: pl.BlockSpec((None, H0, C), lambda b, i: (b, jnp.maximum(i * hb - 1, 0), col))
    nxt = lambda col: pl.BlockSpec((None, H0, C), lambda b, i: (b, jnp.minimum((i + 1) * hb, nhb - 1), col))
    vec = pl.BlockSpec((1, C), lambda b, i: (0, 0))
    return pl.pallas_call(
        functools.partial(_conv_kernel, tl=tl, C=C, KW=KW),
        grid=(B, L // tl),
        in_specs=[main(ca), main(ca + 1), prev(ca), prev(ca + 1), nxt(ca), nxt(ca + 1),
                  pl.BlockSpec((KW, C), lambda b, i: (0, 0)), vec, vec, vec],
        out_specs=pl.BlockSpec((None, tl, C), lambda b, i: (b, i, 0)),
        out_shape=jax.ShapeDtypeStruct((B, L, C), BF16),
        scratch_shapes=[pltpu.VMEM((tl + 2 * H0, C), F32), pltpu.VMEM((tl, C), F32)],
        compiler_params=_params(_ARB2),
        name="conv",
    )(proj3, proj3, proj3, proj3, proj3, proj3, conv_w, conv_b, ln_g, ln_b)


def _merge_kernel(att_ref, cn_ref, wa_ref, wc_ref, ga_ref, gc_ref, o_ref):
    a = jnp.dot(att_ref[...], wa_ref[...], preferred_element_type=F32)
    c = jnp.dot(cn_ref[...], wc_ref[...], preferred_element_type=F32)
    o_ref[...] = (ga_ref[...].astype(F32) * a + gc_ref[...].astype(F32) * c).astype(BF16)


def _merge(att, cn, wa, wc, layer, proj, gate_start, tm, tn):
    M, A = att.shape
    C = cn.shape[1]
    D = wa.shape[2]
    assert gate_start % tn == 0 and D % tn == 0
    ga0 = gate_start // tn
    gc0 = (gate_start + D) // tn
    return pl.pallas_call(
        _merge_kernel,
        grid=(M // tm, D // tn),
        in_specs=[pl.BlockSpec((tm, A), lambda m, n: (m, 0)),
                  pl.BlockSpec((tm, C), lambda m, n: (m, 0)),
                  pl.BlockSpec((None, A, tn), lambda m, n: (layer, 0, n)),
                  pl.BlockSpec((None, C, tn), lambda m, n: (layer, 0, n)),
                  pl.BlockSpec((tm, tn), lambda m, n: (m, ga0 + n)),
                  pl.BlockSpec((tm, tn), lambda m, n: (m, gc0 + n))],
        out_specs=pl.BlockSpec((tm, tn), lambda m, n: (m, n)),
        out_shape=jax.ShapeDtypeStruct((M, D), BF16),
        compiler_params=_params(_ARB2),
        name="merge",
    )(att, cn, wa, wc, proj, proj)


def _final_norm_kernel(h_ref, g_ref, o_ref):
    x = h_ref[...]
    ms = jnp.mean(x * x, axis=-1, keepdims=True)
    o_ref[...] = x * lax.rsqrt(ms + EPS) * g_ref[...]


def _final_norm(h, g, B, L, n_meta, tr):
    D = h.shape[1]
    S = L - n_meta
    assert L % 8 == 0 and n_meta % 8 == 0 and tr % 8 == 0
    return pl.pallas_call(
        _final_norm_kernel,
        grid=(B, S // tr),
        in_specs=[pl.BlockSpec((pl.Element(tr), pl.Element(D)),
                               lambda b, i: (pl.multiple_of(b * L + n_meta + i * tr, 8), 0)),
                  pl.BlockSpec((1, D), lambda b, i: (0, 0))],
        out_specs=pl.BlockSpec((None, tr, D), lambda b, i: (b, i, 0)),
        out_shape=jax.ShapeDtypeStruct((B, S, D), F32),
        compiler_params=_params(_ARB2),
        name="final_norm",
    )(h, g)


def _conv_tile(L):
    for mult in (48, 32, V7X_BF16_SUBLANES):
        if L % mult == 0:
            return _divisor_tile(L, 2100, mult)
    raise ValueError(f"sequence length {L} is not a multiple of {V7X_BF16_SUBLANES}")


def _tiles(M, L, S, D, F, q_width, gate_start, d_in):
    sub = V7X_BF16_SUBLANES
    return dict(
        tm_norm=_divisor_tile(M, 704, sub),
        tm=_divisor_tile(M, 1024, sub),
        tf=_divisor_tile(F, 512, V7X_LANES),
        tn_res=_divisor_tile(D, 1024, V7X_LANES),
        tm_down=_divisor_tile(M, 2100, sub),
        tn_down=_divisor_tile(D, 512, V7X_LANES),
        tn_proj=_divisor_tile(math.gcd(q_width, gate_start, d_in), 1024, V7X_LANES),
        tn_merge=_divisor_tile(math.gcd(D, gate_start), 1024, V7X_LANES),
        tl=_conv_tile(L),
        tr=_divisor_tile(S, 512, 8),
        tq_target=704 if L <= 4096 else 448,
        tk=256,
    )


def _encode(x, meta, rel_bias, wts, final_norm, dims):
    H, dh, V = dims["H"], dims["dh"], dims["V"]
    B, S, D = x.shape
    n_meta = meta.shape[0]
    L = S + n_meta
    M = B * L
    depth = wts["w_in"].shape[0]
    F = wts["w_ffn1_gate"].shape[-1]
    KW, C = wts["conv_w"].shape[1:]
    d_in = wts["w_in"].shape[-1]
    qk_width = H * 2 * dh
    u_start = 2 * qk_width + H * V
    gate_start = u_start + 2 * C
    assert d_in == gate_start + 2 * D

    t = _tiles(M, L, S, D, F, qk_width, gate_start, d_in)
    plan = _AttnPlan(L, t["tq_target"], t["tk"])
    bank = _bias_bank(rel_bias, plan)
    qscale = dh ** -0.5 * LOG2E

    h = jnp.concatenate([jnp.broadcast_to(meta[None].astype(x.dtype), (B, n_meta, D)), x],
                        axis=1).reshape(M, D)
    ssq = _row_ssq(h, t["tm_norm"])
    row = lambda v: v.reshape(1, -1)
    for l in range(depth):
        lam_init = 0.8 - 0.6 * math.exp(-0.3 * l)
        hm = _ffn_up(h, ssq, row(wts["norm_ffn1"][l]), wts["w_ffn1_gate"], wts["w_ffn1_up"], l,
                     t["tm_norm"], t["tf"])
        h, ssq = _mm_res(hm, wts["w_ffn1_down"], l, h, 0.5, t["tm_down"], t["tn_down"])

        proj = _proj(h, ssq, row(wts["norm_mix"][l]), wts["w_in"], l, t["tm_norm"], t["tn_proj"],
                     qk_width, gate_start, qscale)
        proj3 = proj.reshape(B, L, d_in)
        att = _attention(proj3, bank, rel_bias, row(wts["lambda_q1"][l]), row(wts["lambda_k1"][l]),
                         row(wts["lambda_q2"][l]), row(wts["lambda_k2"][l]), row(wts["subln_g"][l]),
                         plan, H, dh, V, lam_init)
        cn = _conv_module(proj3, wts["conv_w"][l], row(wts["conv_b"][l]), row(wts["conv_ln_g"][l]),
                          row(wts["conv_ln_b"][l]), u_start, t["tl"])
        mg = _merge(att.reshape(M, H * V), cn.reshape(M, C), wts["w_attn_out"],
                    wts["w_conv_out"], l, proj, gate_start, t["tm"], t["tn_merge"])
        h, ssq = _mm_res(mg, wts["w_out"], l, h, 1.0, t["tm"], t["tn_res"])

        hm = _ffn_up(h, ssq, row(wts["norm_ffn2"][l]), wts["w_ffn2_gate"], wts["w_ffn2_up"], l,
                     t["tm_norm"], t["tf"])
        h, ssq = _mm_res(hm, wts["w_ffn2_down"], l, h, 0.5, t["tm_down"], t["tn_down"])
    return _final_norm(h, row(final_norm), B, L, n_meta, t["tr"])


_MATMUL_WEIGHTS = ("w_ffn1_gate", "w_ffn1_up", "w_ffn1_down", "w_in", "w_attn_out",
                   "w_conv_out", "w_out", "w_ffn2_gate", "w_ffn2_up", "w_ffn2_down")


def kernel(x_prompt, x_sample, meta_tokens, rel_bias, norm_ffn1, w_ffn1_gate, w_ffn1_up, w_ffn1_down, norm_mix, w_in, lambda_q1, lambda_k1, lambda_q2, lambda_k2, subln_g, w_attn_out, conv_w, conv_b, conv_ln_g, conv_ln_b, w_conv_out, w_out, norm_ffn2, w_ffn2_gate, w_ffn2_up, w_ffn2_down, final_norm):
    wts = dict(norm_ffn1=norm_ffn1, w_ffn1_gate=w_ffn1_gate, w_ffn1_up=w_ffn1_up,
               w_ffn1_down=w_ffn1_down, norm_mix=norm_mix, w_in=w_in, lambda_q1=lambda_q1,
               lambda_k1=lambda_k1, lambda_q2=lambda_q2, lambda_k2=lambda_k2, subln_g=subln_g,
               w_attn_out=w_attn_out, conv_w=conv_w, conv_b=conv_b, conv_ln_g=conv_ln_g,
               conv_ln_b=conv_ln_b, w_conv_out=w_conv_out, w_out=w_out, norm_ffn2=norm_ffn2,
               w_ffn2_gate=w_ffn2_gate, w_ffn2_up=w_ffn2_up, w_ffn2_down=w_ffn2_down)
    for name in _MATMUL_WEIGHTS:
        wts[name] = wts[name].astype(BF16)
    dh = lambda_q1.shape[-1]
    V = subln_g.shape[-1]
    dims = dict(H=rel_bias.shape[1], dh=dh, V=V)
    y_prompt = _encode(x_prompt, meta_tokens, rel_bias, wts, final_norm, dims)
    y_sample = _encode(x_sample, meta_tokens, rel_bias, wts, final_norm, dims)
    return (y_prompt, y_sample)
```

```python
import functools
import math

import jax
import jax.numpy as jnp
from jax import lax
from jax.experimental import pallas as pl
from jax.experimental.pallas import tpu as pltpu

F32 = jnp.float32
BF16 = jnp.bfloat16

EPS = 1e-6
LN_EPS = 1e-5
MAX_DISTANCE = 128
LOG2E = math.log2(math.e)

V7X_LANES = 128
V7X_BF16_SUBLANES = 16
V7X_VMEM_LIMIT_BYTES = 56 * 1024 * 1024
V7X_VMEM_LOOKAHEAD_BYTES = 52 * 1024 * 1024

_ARB2 = ("arbitrary", "arbitrary")
_ARB3 = ("arbitrary", "arbitrary", "arbitrary")


def _params(sem):
    return pltpu.CompilerParams(dimension_semantics=sem,
                                vmem_limit_bytes=V7X_VMEM_LIMIT_BYTES)


def _sigmoid(x):
    return 0.5 * jnp.tanh(0.5 * x) + 0.5


def _divisor_tile(n, target, mult):
    best = None
    for t in range(mult, min(n, target) + 1, mult):
        if n % t == 0:
            best = t
    if best is None:
        raise ValueError(f"no tile for n={n} target={target} mult={mult}")
    return best


_NORM_UNROLL = 4


def _lookahead_rows(n_row_blocks):
    return lambda m, n: (jnp.minimum(m + jnp.minimum(n, 1), n_row_blocks - 1), 0)


def _norm_chunk(h_ref, ssq_ref, g, xn_ref, chunk):
    rows = V7X_BF16_SUBLANES
    r = pl.multiple_of(chunk * rows, rows)
    x = h_ref[pl.ds(r, rows), :]
    ms = jnp.sum(ssq_ref[pl.ds(r, rows), :], axis=1, keepdims=True) * (1.0 / x.shape[1])
    xn_ref[pl.ds(r, rows), :] = (x * lax.rsqrt(ms + EPS) * g).astype(BF16)


def _norm_block(h_ref, ssq_ref, g_ref, xn_ref):
    g = g_ref[...]

    def body(i, c):
        _norm_chunk(h_ref, ssq_ref, g, xn_ref, i)
        return c

    lax.fori_loop(0, h_ref.shape[0] // V7X_BF16_SUBLANES, body, 0, unroll=_NORM_UNROLL)


def _norm_share(h_ref, ssq_ref, g_ref, xn_ref, step, n_steps):
    n_chunks = h_ref.shape[0] // V7X_BF16_SUBLANES
    per_step = -(-n_chunks // n_steps)
    g = g_ref[...]
    for u in range(per_step):
        _norm_chunk(h_ref, ssq_ref, g, xn_ref,
                    jnp.minimum(step * per_step + u, n_chunks - 1))


def _normed_matmul_steps(h_ref, ssq_ref, g_ref, xn_bufs, step_body):
    m = pl.program_id(0)
    n = pl.program_id(1)
    n_steps = pl.num_programs(1) - 1

    if len(xn_bufs) == 1:
        @pl.when(n == 0)
        def _():
            _norm_block(h_ref, ssq_ref, g_ref, xn_bufs[0])

        step_body(xn_bufs[0])
        return

    @pl.when(jnp.logical_and(m == 0, n == 0))
    def _():
        _norm_block(h_ref, ssq_ref, g_ref, xn_bufs[0])
        step_body(xn_bufs[0])

    for parity in (0, 1):
        mine, other = xn_bufs[parity], xn_bufs[1 - parity]

        @pl.when(jnp.logical_and(jnp.logical_and(m > 0, n == 0), m % 2 == parity))
        def _(mine=mine):
            step_body(mine)

        @pl.when(jnp.logical_and(n > 0, m % 2 == parity))
        def _(mine=mine, other=other):
            _norm_share(h_ref, ssq_ref, g_ref, other, n - 1, n_steps)
            step_body(mine)


def _row_ssq_kernel(h_ref, o_ref):
    x = h_ref[...]
    sq = x * x
    part = sq[:, 0:V7X_LANES]
    for t in range(1, x.shape[1] // V7X_LANES):
        part = part + sq[:, t * V7X_LANES:(t + 1) * V7X_LANES]
    o_ref[...] = part


def _row_ssq(h, tr):
    M, D = h.shape
    return pl.pallas_call(
        _row_ssq_kernel,
        grid=(M // tr,),
        in_specs=[pl.BlockSpec((tr, D), lambda m: (m, 0))],
        out_specs=pl.BlockSpec((tr, V7X_LANES), lambda m: (m, 0)),
        out_shape=jax.ShapeDtypeStruct((M, V7X_LANES), F32),
        compiler_params=_params(("arbitrary",)),
        name="row_ssq",
    )(h)


def _ffn_up_kernel(h_ref, ssq_ref, g_ref, wg_ref, wu_ref, o_ref, *xn_bufs):
    def step(xn_ref):
        xn = xn_ref[...]
        a = jnp.dot(xn, wg_ref[...], preferred_element_type=F32)
        b = jnp.dot(xn, wu_ref[...], preferred_element_type=F32)
        o_ref[...] = (a * _sigmoid(a) * b).astype(BF16)

    _normed_matmul_steps(h_ref, ssq_ref, g_ref, xn_bufs, step)


def _ffn_up(h, ssq, g, wg, wu, layer, tm, tf):
    M, D = h.shape
    F = wg.shape[2]
    fixed = 2 * tm * D * 4 + 2 * 2 * D * tf * 2 + 2 * tm * tf * 2 + 4 * tm * tf * 4
    lookahead = F // tf >= 2 and fixed + 2 * tm * D * 2 <= V7X_VMEM_LOOKAHEAD_BYTES
    rows = _lookahead_rows(M // tm) if lookahead else (lambda m, f: (m, 0))
    return pl.pallas_call(
        _ffn_up_kernel,
        grid=(M // tm, F // tf),
        in_specs=[pl.BlockSpec((tm, D), rows),
                  pl.BlockSpec((tm, V7X_LANES), rows),
                  pl.BlockSpec((1, D), lambda m, f: (0, 0)),
                  pl.BlockSpec((None, D, tf), lambda m, f: (layer, 0, f)),
                  pl.BlockSpec((None, D, tf), lambda m, f: (layer, 0, f))],
        out_specs=pl.BlockSpec((tm, tf), lambda m, f: (m, f)),
        out_shape=jax.ShapeDtypeStruct((M, F), BF16),
        scratch_shapes=[pltpu.VMEM((tm, D), BF16)] * (2 if lookahead else 1),
        compiler_params=_params(_ARB2),
        name="ffn_up",
    )(h, ssq, g, wg, wu)


def _mm_res_kernel(x_ref, w_ref, h_ref, o_ref, ssq_ref, *, scale):
    @pl.when(pl.program_id(1) == 0)
    def _():
        ssq_ref[...] = jnp.zeros(ssq_ref.shape, F32)

    acc = jnp.dot(x_ref[...], w_ref[...], preferred_element_type=F32)
    y = h_ref[...] + scale * acc
    o_ref[...] = y
    sq = y * y
    part = sq[:, 0:V7X_LANES]
    for t in range(1, y.shape[1] // V7X_LANES):
        part = part + sq[:, t * V7X_LANES:(t + 1) * V7X_LANES]
    ssq_ref[...] = ssq_ref[...] + part


def _mm_res(x, w, layer, h, scale, tm, tn):
    M, K = x.shape
    N = w.shape[2]
    return pl.pallas_call(
        functools.partial(_mm_res_kernel, scale=scale),
        grid=(M // tm, N // tn),
        in_specs=[pl.BlockSpec((tm, K), lambda m, n: (m, 0)),
                  pl.BlockSpec((None, K, tn), lambda m, n: (layer, 0, n)),
                  pl.BlockSpec((tm, tn), lambda m, n: (m, n))],
        out_specs=[pl.BlockSpec((tm, tn), lambda m, n: (m, n)),
                   pl.BlockSpec((tm, V7X_LANES), lambda m, n: (m, 0))],
        out_shape=[jax.ShapeDtypeStruct((M, N), F32),
                   jax.ShapeDtypeStruct((M, V7X_LANES), F32)],
        compiler_params=_params(_ARB2),
        name="mm_res",
    )(x, w, h)


def _proj_kernel(h_ref, ssq_ref, g_ref, w_ref, o_ref, xn_ref, *, n_q, n_lin, qscale):
    n = pl.program_id(1)

    @pl.when(n == 0)
    def _():
        _norm_block(h_ref, ssq_ref, g_ref, xn_ref)

    def acc():
        return jnp.dot(xn_ref[...], w_ref[...], preferred_element_type=F32)

    @pl.when(n < n_q)
    def _():
        o_ref[...] = (acc() * qscale).astype(BF16)

    @pl.when(jnp.logical_and(n >= n_q, n < n_lin))
    def _():
        o_ref[...] = acc().astype(BF16)

    @pl.when(n >= n_lin)
    def _():
        o_ref[...] = _sigmoid(acc()).astype(BF16)


def _proj(h, ssq, g, w, layer, tm, tn, q_width, gate_start, qscale):
    M, D = h.shape
    N = w.shape[2]
    assert q_width % tn == 0 and gate_start % tn == 0
    return pl.pallas_call(
        functools.partial(_proj_kernel, n_q=q_width // tn, n_lin=gate_start // tn,
                          qscale=qscale),
        grid=(M // tm, N // tn),
        in_specs=[pl.BlockSpec((tm, D), lambda m, n: (m, 0)),
                  pl.BlockSpec((tm, V7X_LANES), lambda m, n: (m, 0)),
                  pl.BlockSpec((1, D), lambda m, n: (0, 0)),
                  pl.BlockSpec((None, D, tn), lambda m, n: (layer, 0, n))],
        out_specs=pl.BlockSpec((tm, tn), lambda m, n: (m, n)),
        out_shape=jax.ShapeDtypeStruct((M, N), BF16),
        scratch_shapes=[pltpu.VMEM((tm, D), BF16)],
        compiler_params=_params(_ARB2),
        name="proj",
    )(h, ssq, g, w)


def _t5_bucket(rel, n_buckets):
    nb = n_buckets // 2
    max_exact = nb // 2
    ret = (rel > 0).astype(jnp.int32) * nb
    n = jnp.abs(rel)
    nf = jnp.maximum(n, 1).astype(jnp.float32)
    large = max_exact + (jnp.log(nf / max_exact) / math.log(MAX_DISTANCE / max_exact)
                         * (nb - max_exact)).astype(jnp.int32)
    large = jnp.minimum(large, nb - 1)
    return ret + jnp.where(n < max_exact, n, large)


_ATTN_TILE_ROWS = 432
_ATTN_UNROLL = 4
_ATTN_FAR_UNROLL = 16


class _AttnPlan:
    def __init__(self, L, tq_target, tk):
        self.L = L
        self.tk = tk
        self.tq = _divisor_tile(L, tq_target, V7X_BF16_SUBLANES)
        self.nq = L // self.tq
        self.nk = L // tk
        self.n_tail = L - self.nk * tk
        assert self.n_tail % V7X_BF16_SUBLANES == 0
        self.jl = [self.far_left(i * self.tq) for i in range(self.nq)]
        self.jr = [self.far_right(i * self.tq) for i in range(self.nq)]
        self.n_near = max(1, max(r - l for l, r in zip(self.jl, self.jr)))
        units = self.tq // V7X_BF16_SUBLANES
        n = max(1, round(self.tq / _ATTN_TILE_ROWS))
        sizes = [(units // n + (1 if t < units % n else 0)) * V7X_BF16_SUBLANES for t in range(n)]
        self.row_tiles = tuple((sum(sizes[:t]), sizes[t]) for t in range(n))

    def far_left(self, q0):
        return _near_range(q0, self.tq, self.tk, self.nk, min, max)[0]

    def far_right(self, q0):
        return _near_range(q0, self.tq, self.tk, self.nk, min, max)[1]


def _near_range(q0, tq, tk, nk, minimum, maximum):
    u = _ATTN_UNROLL
    lo = maximum(q0 - MAX_DISTANCE + 1, 0) // tk // u * u
    hi = (q0 + tq - 1 + MAX_DISTANCE + tk - 1) // tk
    hi = minimum((hi + u - 1) // u * u, nk)
    return minimum(lo, nk), hi


def _bias_vec_kernel(rb_ref, bk_ref, o_ref, *, n_buckets):
    bk = bk_ref[...]
    for h in range(o_ref.shape[0]):
        acc = jnp.zeros(bk.shape, F32)
        for b in range(n_buckets):
            acc = jnp.where(bk == b, rb_ref[b, h] * LOG2E, acc)
        o_ref[h:h + 1, :] = acc


def _bias_vectors(rel_bias, L, length):
    n_buckets, H = rel_bias.shape
    rel = jnp.arange(length, dtype=jnp.int32)[None, :] - (L - 1)
    return pl.pallas_call(
        functools.partial(_bias_vec_kernel, n_buckets=n_buckets),
        in_specs=[pl.BlockSpec(memory_space=pltpu.SMEM),
                  pl.BlockSpec((1, length), lambda: (0, 0))],
        out_specs=pl.BlockSpec((H, length), lambda: (0, 0)),
        out_shape=jax.ShapeDtypeStruct((H, length), F32),
        name="bias_vectors",
    )(rel_bias, _t5_bucket(rel, n_buckets))


_BANK_ROWS = 8
_BANK_UNROLL = 6


def _bank_kernel(rb_ref, bv_ref, o_ref, *, n_buckets, tq, tk, nk, n_near, n_tail, L):
    i = pl.program_id(0)
    t = pl.program_id(1)
    H = o_ref.shape[0]
    q0 = i * tq
    lo, _ = _near_range(q0, tq, tk, nk, jnp.minimum, jnp.maximum)
    is_chunk = t < n_near
    kstart = jnp.where(is_chunk, (lo + t) * tk, nk * tk)
    n_valid = jnp.where(is_chunk, tk, n_tail)
    all_left = jnp.logical_and(is_chunk, kstart + tk - 1 - q0 <= -MAX_DISTANCE)
    all_right = jnp.logical_and(is_chunk, kstart - (q0 + tq - 1) >= MAX_DISTANCE)

    def fill(bucket):
        for h in range(H):
            o_ref[h] = jnp.full((tq, tk), rb_ref[bucket, h] * LOG2E, F32)

    @pl.when(all_left)
    def _():
        fill(n_buckets // 2 - 1)

    @pl.when(all_right)
    def _():
        fill(n_buckets - 1)

    @pl.when(jnp.logical_not(jnp.logical_or(all_left, all_right)))
    def _():
        R = _BANK_ROWS
        win_w = tk + 2 * V7X_LANES
        col = lax.broadcasted_iota(jnp.int32, (R, tk), 1)

        def body(g, carry):
            r0 = g * R
            off = kstart - (q0 + r0 + R - 1) + (L - 1)
            off_al = pl.multiple_of(off // V7X_LANES * V7X_LANES, V7X_LANES)
            shift = (win_w - (R - 1) - (off - off_al)) % win_w
            for h in range(H):
                win = jnp.broadcast_to(bv_ref[h:h + 1, pl.ds(off_al, win_w)], (R, win_w))
                tile = pltpu.roll(win, shift, 1, stride=1, stride_axis=0)[:, :tk]
                o_ref[h, pl.ds(pl.multiple_of(r0, R), R), :] = jnp.where(col < n_valid, tile,
                                                                          -jnp.inf)
            return carry

        lax.fori_loop(0, tq // R, body, 0, unroll=_BANK_UNROLL)


def _bias_bank(rel_bias, plan):
    n_buckets, H = rel_bias.shape
    tq, tk, nq, nt = plan.tq, plan.tk, plan.nq, plan.n_near + 1
    length = -(-(plan.L + (plan.nk + nt) * tk + tk + 2 * V7X_LANES) // V7X_LANES) * V7X_LANES
    bvec = _bias_vectors(rel_bias, plan.L, length)
    return pl.pallas_call(
        functools.partial(_bank_kernel, n_buckets=n_buckets, tq=tq, tk=tk, nk=plan.nk,
                          n_near=plan.n_near, n_tail=plan.n_tail, L=plan.L),
        grid=(nq, nt),
        in_specs=[pl.BlockSpec(memory_space=pltpu.SMEM),
                  pl.BlockSpec((H, length), lambda i, t: (0, 0))],
        out_specs=pl.BlockSpec((H, None, None, tq, tk), lambda i, t: (0, i, t, 0, 0)),
        out_shape=jax.ShapeDtypeStruct((H, nq, nt, tq, tk), F32),
        compiler_params=_params(_ARB2),
        name="bias_bank",
    )(rel_bias, bvec)


def _attn_kernel(rb_ref, lam0_ref, q_ref, k_ref, v_ref, bank_ref, lq1_ref, lk1_ref, lq2_ref,
                 lk2_ref, sg_ref, o_ref, m1, l1, a1, m2, l2, a2, kt, vt, *,
                 tq, tk, nk, n_tail, n_near, dh, n_buckets, row_tiles):
    h = pl.program_id(0)
    i = pl.program_id(1)
    lam_init = lam0_ref[0]
    out_gain = lam0_ref[1]
    V = v_ref.shape[-1]
    q0 = i * tq
    jl, jr = _near_range(q0, tq, tk, nk, jnp.minimum, jnp.maximum)
    c_left = rb_ref[n_buckets // 2 - 1, h] * LOG2E
    c_right = rb_ref[n_buckets - 1, h] * LOG2E

    for m_ref, l_ref, a_ref in ((m1, l1, a1), (m2, l2, a2)):
        m_ref[...] = jnp.full(m_ref.shape, -jnp.inf, F32)
        l_ref[...] = jnp.zeros(l_ref.shape, F32)
        a_ref[...] = jnp.zeros(a_ref.shape, F32)

    maps = ((slice(0, dh), m1, l1, a1), (slice(dh, 2 * dh), m2, l2, a2))

    def update(rows, kmap, vc, bias, qcols, m_ref, l_ref, a_ref):
        s = lax.dot_general(q_ref[rows, qcols], kmap, (((1,), (1,)), ((), ())),
                            preferred_element_type=F32)
        if bias is not None:
            s = s + bias
        lane_tiles = s.shape[1] // V7X_LANES
        m_prev = m_ref[rows, :]
        m_next = jnp.maximum(m_prev, jnp.max(s, axis=1, keepdims=True))
        p = jnp.exp2(s - jnp.tile(m_next, (1, lane_tiles)))
        alpha = jnp.exp2(m_prev - m_next)
        psum = p[:, 0:V7X_LANES]
        for t in range(1, lane_tiles):
            psum = psum + p[:, t * V7X_LANES:(t + 1) * V7X_LANES]
        l_ref[rows, :] = alpha * l_ref[rows, :] + psum
        m_ref[rows, :] = m_next
        pv = jnp.dot(p.astype(BF16), vc, preferred_element_type=F32)
        a_ref[rows, :] = a_ref[rows, :] * jnp.tile(alpha, (1, V // V7X_LANES)) + pv

    def chunk(kv_rows, width, bias_tile, k_src, v_src):
        for r0, rq in row_tiles:
            rows = slice(r0, r0 + rq)
            for qcols, m_ref, l_ref, a_ref in maps:
                bias = None if bias_tile is None else bank_ref[bias_tile, rows, 0:width]
                update(rows, k_src[kv_rows, qcols], v_src[kv_rows, :], bias,
                       qcols, m_ref, l_ref, a_ref)

    def far_chunk(j):
        chunk(pl.ds(pl.multiple_of(j * tk, tk), tk), tk, None, k_ref, v_ref)

    def near_chunk(j):
        chunk(pl.ds(pl.multiple_of(j * tk, tk), tk), tk, j - jl, k_ref, v_ref)

    def sweep(lo, hi, one, unrolls):
        for unroll in unrolls:
            def group(t, carry, lo=lo, unroll=unroll):
                for u in range(unroll):
                    one(lo + unroll * t + u)
                return carry

            n_groups = (hi - lo) // unroll
            lax.fori_loop(0, n_groups, group, 0)
            lo = lo + unroll * n_groups

    def shift_max(delta):
        m1[...] = m1[...] + delta
        m2[...] = m2[...] + delta

    far_unrolls = (_ATTN_FAR_UNROLL, _ATTN_FAR_UNROLL // 2, _ATTN_UNROLL, 1)
    sweep(0, jl, far_chunk, far_unrolls)
    shift_max(c_left)
    sweep(jl, jr, near_chunk, (_ATTN_UNROLL, 1))
    shift_max(-c_right)
    sweep(jr, nk, far_chunk, far_unrolls)
    shift_max(c_right)

    if n_tail:
        kt[...] = jnp.zeros(kt.shape, BF16)
        vt[...] = jnp.zeros(vt.shape, BF16)
        kt[0:n_tail, :] = k_ref[nk * tk:nk * tk + n_tail, :]
        vt[0:n_tail, :] = v_ref[nk * tk:nk * tk + n_tail, :]
        chunk(slice(0, kt.shape[0]), kt.shape[0], n_near, kt, vt)

    lam = (jnp.exp(jnp.sum(lq1_ref[...] * lk1_ref[...], axis=-1, keepdims=True))
           - jnp.exp(jnp.sum(lq2_ref[...] * lk2_ref[...], axis=-1, keepdims=True))
           + lam_init)
    for r0, rq in row_tiles:
        rows = slice(r0, r0 + rq)
        w1 = 1.0 / jnp.sum(l1[rows, :], axis=1, keepdims=True)
        w2 = lam / jnp.sum(l2[rows, :], axis=1, keepdims=True)
        att = a1[rows, :] * w1 - a2[rows, :] * w2
        ms = jnp.mean(att * att, axis=-1, keepdims=True)
        y = att * lax.rsqrt(ms + EPS) * sg_ref[...] * out_gain
        o_ref[rows, :] = y.astype(BF16)


def _attention(proj3, bank, rel_bias, lq1, lk1, lq2, lk2, subln_g, plan, H, dh, V, lam_init):
    B, L, _ = proj3.shape
    assert V == 2 * dh and V % V7X_LANES == 0 and plan.tk % V7X_LANES == 0
    tq, tk = plan.tq, plan.tk
    nt = plan.n_near + 1
    tail_w = max(V7X_LANES, -(-plan.n_tail // V7X_LANES) * V7X_LANES)
    kern = functools.partial(
        _attn_kernel, tq=tq, tk=tk, nk=plan.nk, n_tail=plan.n_tail, n_near=plan.n_near,
        dh=dh, n_buckets=rel_bias.shape[0], row_tiles=plan.row_tiles)
    vec = lambda w: pl.BlockSpec((1, w), lambda h, i, b: (0, 0))
    lam0 = jnp.asarray([lam_init, 1.0 - lam_init], F32)
    return pl.pallas_call(
        kern,
        grid=(H, plan.nq, B),
        in_specs=[pl.BlockSpec(memory_space=pltpu.SMEM),
                  pl.BlockSpec(memory_space=pltpu.SMEM),
                  pl.BlockSpec((None, tq, V), lambda h, i, b: (b, i, h)),
                  pl.BlockSpec((None, L, V), lambda h, i, b: (b, 0, H + h)),
                  pl.BlockSpec((None, L, V), lambda h, i, b: (b, 0, 2 * H + h)),
                  pl.BlockSpec((None, None, nt, tq, tk), lambda h, i, b: (h, i, 0, 0, 0)),
                  vec(dh), vec(dh), vec(dh), vec(dh), vec(V)],
        out_specs=pl.BlockSpec((None, tq, V), lambda h, i, b: (b, i, h)),
        out_shape=jax.ShapeDtypeStruct((B, L, H * V), BF16),
        scratch_shapes=[pltpu.VMEM((tq, V7X_LANES), F32), pltpu.VMEM((tq, V7X_LANES), F32),
                        pltpu.VMEM((tq, V), F32),
                        pltpu.VMEM((tq, V7X_LANES), F32), pltpu.VMEM((tq, V7X_LANES), F32),
                        pltpu.VMEM((tq, V), F32),
                        pltpu.VMEM((tail_w, V), BF16), pltpu.VMEM((tail_w, V), BF16)],
        compiler_params=_params(_ARB3),
        name="attn",
    )(rel_bias, lam0, proj3, proj3, proj3, bank, lq1, lk1, lq2, lk2, subln_g)


_CONV_HALO = 16
_CONV_LANES = 256


def _conv_rows(tl):
    return _divisor_tile(tl, 48, V7X_BF16_SUBLANES)


def _conv_kernel(ua_ref, ub_ref, pa_ref, pb_ref, na_ref, nb_ref, w_ref, cb_ref, lg_ref, lb_ref,
                 o_ref, cpad, y_scr, *, tl, C, KW):
    i = pl.program_id(1)
    last = pl.num_programs(1) - 1
    H0 = _CONV_HALO
    RC = _conv_rows(tl)
    WIN = RC + 2 * H0
    off = H0 - KW // 2

    def glu(a, b):
        return a.astype(F32) * _sigmoid(b.astype(F32))

    cpad[0:H0, :] = jnp.where(i > 0, glu(pa_ref[...], pb_ref[...]), 0.0)
    cpad[H0 + tl:H0 + tl + H0, :] = jnp.where(i < last, glu(na_ref[...], nb_ref[...]), 0.0)

    def glu_body(c, carry):
        r = pl.multiple_of(c * RC, RC)
        cpad[pl.ds(H0 + r, RC), :] = glu(ua_ref[pl.ds(r, RC), :], ub_ref[pl.ds(r, RC), :])
        return carry

    lax.fori_loop(0, tl // RC, glu_body, 0)

    cg = min(C, _CONV_LANES)

    def conv_chunk(c):
        r = pl.multiple_of(c * RC, RC)
        for g in range(C // cg):
            cols = slice(g * cg, (g + 1) * cg)
            win = cpad[pl.ds(r, WIN), cols]
            acc = jnp.zeros((RC, cg), F32)
            for sh in range(8):
                taps = [t for t in range(KW) if (t + off) % 8 == sh]
                if not taps:
                    continue
                wsh = win if sh == 0 else pltpu.roll(win, WIN - sh, axis=0)
                for t in taps:
                    a = (t + off) // 8
                    acc = acc + wsh[8 * a:8 * a + RC, :] * w_ref[t:t + 1, cols]
            y_scr[pl.ds(r, RC), cols] = acc + cb_ref[:, cols]

    def ln_chunk(c):
        r = pl.multiple_of(c * RC, RC)
        x = y_scr[pl.ds(r, RC), :]
        mu = jnp.mean(x, axis=-1, keepdims=True)
        var = jnp.mean(jnp.square(x - mu), axis=-1, keepdims=True)
        y = (x - mu) * lax.rsqrt(var + LN_EPS) * lg_ref[...] + lb_ref[...]
        o_ref[pl.ds(r, RC), :] = (y * _sigmoid(y)).astype(BF16)

    conv_chunk(0)

    def body(c, carry):
        conv_chunk(c)
        ln_chunk(c - 1)
        return carry

    n_chunks = tl // RC
    lax.fori_loop(1, n_chunks, body, 0)
    ln_chunk(n_chunks - 1)


def _conv_module(proj3, conv_w, conv_b, ln_g, ln_b, u_start, tl):
    B, L, _ = proj3.shape
    KW, C = conv_w.shape
    H0 = _CONV_HALO
    assert KW // 2 <= H0 and KW - 1 + H0 - KW // 2 < 2 * H0 + 1
    assert u_start % C == 0 and L % tl == 0 and tl % V7X_BF16_SUBLANES == 0 and L % H0 == 0
    ca = u_start // C
    hb = tl // H0
    nhb = L // H0
    main = lambda col: pl.BlockSpec((None, tl, C), lambda b, i: (b, i, col))
    prev = lambda col: pl.BlockSpec((None, H0, C), lambda b, i: (b, jnp.maximum(i * hb - 1, 0), col))
    nxt = lambda col: pl.BlockSpec((None, H0, C), lambda b, i: (b, jnp.minimum((i + 1) * hb, nhb - 1), col))
    vec = pl.BlockSpec((1, C), lambda b, i: (0, 0))
    return pl.pallas_call(
        functools.partial(_conv_kernel, tl=tl, C=C, KW=KW),
        grid=(B, L // tl),
        in_specs=[main(ca), main(ca + 1), prev(ca), prev(ca + 1), nxt(ca), nxt(ca + 1),
                  pl.BlockSpec((KW, C), lambda b, i: (0, 0)), vec, vec, vec],
        out_specs=pl.BlockSpec((None, tl, C), lambda b, i: (b, i, 0)),
        out_shape=jax.ShapeDtypeStruct((B, L, C), BF16),
        scratch_shapes=[pltpu.VMEM((tl + 2 * H0, C), F32), pltpu.VMEM((tl, C), F32)],
        compiler_params=_params(_ARB2),
        name="conv",
    )(proj3, proj3, proj3, proj3, proj3, proj3, conv_w, conv_b, ln_g, ln_b)


def _merge_kernel(att_ref, cn_ref, wa_ref, wc_ref, ga_ref, gc_ref, o_ref):
    a = jnp.dot(att_ref[...], wa_ref[...], preferred_element_type=F32)
    c = jnp.dot(cn_ref[...], wc_ref[...], preferred_element_type=F32)
    o_ref[...] = (ga_ref[...].astype(F32) * a + gc_ref[...].astype(F32) * c).astype(BF16)


def _merge(att, cn, wa, wc, layer, proj, gate_start, tm, tn):
    M, A = att.shape
    C = cn.shape[1]
    D = wa.shape[2]
    assert gate_start % tn == 0 and D % tn == 0
    ga0 = gate_start // tn
    gc0 = (gate_start + D) // tn
    return pl.pallas_call(
        _merge_kernel,
        grid=(M // tm, D // tn),
        in_specs=[pl.BlockSpec((tm, A), lambda m, n: (m, 0)),
                  pl.BlockSpec((tm, C), lambda m, n: (m, 0)),
                  pl.BlockSpec((None, A, tn), lambda m, n: (layer, 0, n)),
                  pl.BlockSpec((None, C, tn), lambda m, n: (layer, 0, n)),
                  pl.BlockSpec((tm, tn), lambda m, n: (m, ga0 + n)),
                  pl.BlockSpec((tm, tn), lambda m, n: (m, gc0 + n))],
        out_specs=pl.BlockSpec((tm, tn), lambda m, n: (m, n)),
        out_shape=jax.ShapeDtypeStruct((M, D), BF16),
        compiler_params=_params(_ARB2),
        name="merge",
    )(att, cn, wa, wc, proj, proj)


def _final_norm_kernel(h_ref, g_ref, o_ref):
    x = h_ref[...]
    ms = jnp.mean(x * x, axis=-1, keepdims=True)
    o_ref[...] = x * lax.rsqrt(ms + EPS) * g_ref[...]


def _final_norm(h, g, B, L, n_meta, tr):
    D = h.shape[1]
    S = L - n_meta
    assert L % 8 == 0 and n_meta % 8 == 0 and tr % 8 == 0
    return pl.pallas_call(
        _final_norm_kernel,
        grid=(B, S // tr),
        in_specs=[pl.BlockSpec((pl.Element(tr), pl.Element(D)),
                               lambda b, i: (pl.multiple_of(b * L + n_meta + i * tr, 8), 0)),
                  pl.BlockSpec((1, D), lambda b, i: (0, 0))],
        out_specs=pl.BlockSpec((None, tr, D), lambda b, i: (b, i, 0)),
        out_shape=jax.ShapeDtypeStruct((B, S, D), F32),
        compiler_params=_params(_ARB2),
        name="final_norm",
    )(h, g)


def _conv_tile(L):
    for mult in (48, 32, V7X_BF16_SUBLANES):
        if L % mult == 0:
            return _divisor_tile(L, 2100, mult)
    raise ValueError(f"sequence length {L} is not a multiple of {V7X_BF16_SUBLANES}")


def _tiles(M, L, S, D, F, q_width, gate_start, d_in):
    sub = V7X_BF16_SUBLANES
    return dict(
        tm_norm=_divisor_tile(M, 704, sub),
        tm=_divisor_tile(M, 1024, sub),
        tf=_divisor_tile(F, 512, V7X_LANES),
        tn_res=_divisor_tile(D, 1024, V7X_LANES),
        tm_down=_divisor_tile(M, 2100, sub),
        tn_down=_divisor_tile(D, 512, V7X_LANES),
        tn_proj=_divisor_tile(math.gcd(q_width, gate_start, d_in), 1024, V7X_LANES),
        tn_merge=_divisor_tile(math.gcd(D, gate_start), 1024, V7X_LANES),
        tl=_conv_tile(L),
        tr=_divisor_tile(S, 512, 8),
        tq_target=704 if L <= 4096 else 448,
        tk=256,
    )


def _encode(x, meta, rel_bias, wts, final_norm, dims):
    H, dh, V = dims["H"], dims["dh"], dims["V"]
    B, S, D = x.shape
    n_meta = meta.shape[0]
    L = S + n_meta
    M = B * L
    depth = wts["w_in"].shape[0]
    F = wts["w_ffn1_gate"].shape[-1]
    KW, C = wts["conv_w"].shape[1:]
    d_in = wts["w_in"].shape[-1]
    qk_width = H * 2 * dh
    u_start = 2 * qk_width + H * V
    gate_start = u_start + 2 * C
    assert d_in == gate_start + 2 * D

    t = _tiles(M, L, S, D, F, qk_width, gate_start, d_in)
    plan = _AttnPlan(L, t["tq_target"], t["tk"])
    bank = _bias_bank(rel_bias, plan)
    qscale = dh ** -0.5 * LOG2E

    h = jnp.concatenate([jnp.broadcast_to(meta[None].astype(x.dtype), (B, n_meta, D)), x],
                        axis=1).reshape(M, D)
    ssq = _row_ssq(h, t["tm_norm"])
    row = lambda v: v.reshape(1, -1)
    for l in range(depth):
        lam_init = 0.8 - 0.6 * math.exp(-0.3 * l)
        hm = _ffn_up(h, ssq, row(wts["norm_ffn1"][l]), wts["w_ffn1_gate"], wts["w_ffn1_up"], l,
                     t["tm_norm"], t["tf"])
        h, ssq = _mm_res(hm, wts["w_ffn1_down"], l, h, 0.5, t["tm_down"], t["tn_down"])

        proj = _proj(h, ssq, row(wts["norm_mix"][l]), wts["w_in"], l, t["tm_norm"], t["tn_proj"],
                     qk_width, gate_start, qscale)
        proj3 = proj.reshape(B, L, d_in)
        att = _attention(proj3, bank, rel_bias, row(wts["lambda_q1"][l]), row(wts["lambda_k1"][l]),
                         row(wts["lambda_q2"][l]), row(wts["lambda_k2"][l]), row(wts["subln_g"][l]),
                         plan, H, dh, V, lam_init)
        cn = _conv_module(proj3, wts["conv_w"][l], row(wts["conv_b"][l]), row(wts["conv_ln_g"][l]),
                          row(wts["conv_ln_b"][l]), u_start, t["tl"])
        mg = _merge(att.reshape(M, H * V), cn.reshape(M, C), wts["w_attn_out"],
                    wts["w_conv_out"], l, proj, gate_start, t["tm"], t["tn_merge"])
        h, ssq = _mm_res(mg, wts["w_out"], l, h, 1.0, t["tm"], t["tn_res"])

        hm = _ffn_up(h, ssq, row(wts["norm_ffn2"][l]), wts["w_ffn2_gate"], wts["w_ffn2_up"], l,
                     t["tm_norm"], t["tf"])
        h, ssq = _mm_res(hm, wts["w_ffn2_down"], l, h, 0.5, t["tm_down"], t["tn_down"])
    return _final_norm(h, row(final_norm), B, L, n_meta, t["tr"])


_MATMUL_WEIGHTS = ("w_ffn1_gate", "w_ffn1_up", "w_ffn1_down", "w_in", "w_attn_out",
                   "w_conv_out", "w_out", "w_ffn2_gate", "w_ffn2_up", "w_ffn2_down")


def kernel(x_prompt, x_sample, meta_tokens, rel_bias, norm_ffn1, w_ffn1_gate, w_ffn1_up, w_ffn1_down, norm_mix, w_in, lambda_q1, lambda_k1, lambda_q2, lambda_k2, subln_g, w_attn_out, conv_w, conv_b, conv_ln_g, conv_ln_b, w_conv_out, w_out, norm_ffn2, w_ffn2_gate, w_ffn2_up, w_ffn2_down, final_norm):
    wts = dict(norm_ffn1=norm_ffn1, w_ffn1_gate=w_ffn1_gate, w_ffn1_up=w_ffn1_up,
               w_ffn1_down=w_ffn1_down, norm_mix=norm_mix, w_in=w_in, lambda_q1=lambda_q1,
               lambda_k1=lambda_k1, lambda_q2=lambda_q2, lambda_k2=lambda_k2, subln_g=subln_g,
               w_attn_out=w_attn_out, conv_w=conv_w, conv_b=conv_b, conv_ln_g=conv_ln_g,
               conv_ln_b=conv_ln_b, w_conv_out=w_conv_out, w_out=w_out, norm_ffn2=norm_ffn2,
               w_ffn2_gate=w_ffn2_gate, w_ffn2_up=w_ffn2_up, w_ffn2_down=w_ffn2_down)
    for name in _MATMUL_WEIGHTS:
        wts[name] = wts[name].astype(BF16)
    dh = lambda_q1.shape[-1]
    V = subln_g.shape[-1]
    dims = dict(H=rel_bias.shape[1], dh=dh, V=V)
    y_prompt = _encode(x_prompt, meta_tokens, rel_bias, wts, final_norm, dims)
    y_sample = _encode(x_sample, meta_tokens, rel_bias, wts, final_norm, dims)
    return (y_prompt, y_sample)
```

```python
import functools
import math

import jax
import jax.numpy as jnp
from jax import lax
from jax.experimental import pallas as pl
from jax.experimental.pallas import tpu as pltpu

F32 = jnp.float32
BF16 = jnp.bfloat16

EPS = 1e-6
LN_EPS = 1e-5
MAX_DISTANCE = 128
LOG2E = math.log2(math.e)

V7X_LANES = 128
V7X_BF16_SUBLANES = 16
V7X_VMEM_LIMIT_BYTES = 56 * 1024 * 1024
V7X_VMEM_LOOKAHEAD_BYTES = 52 * 1024 * 1024

_ARB2 = ("arbitrary", "arbitrary")
_ARB3 = ("arbitrary", "arbitrary", "arbitrary")


def _params(sem):
    return pltpu.CompilerParams(dimension_semantics=sem,
                                vmem_limit_bytes=V7X_VMEM_LIMIT_BYTES)


def _sigmoid(x):
    return 0.5 * jnp.tanh(0.5 * x) + 0.5


def _divisor_tile(n, target, mult):
    best = None
    for t in range(mult, min(n, target) + 1, mult):
        if n % t == 0:
            best = t
    if best is None:
        raise ValueError(f"no tile for n={n} target={target} mult={mult}")
    return best


_NORM_UNROLL = 4


def _lookahead_rows(n_row_blocks):
    return lambda m, n: (jnp.minimum(m + jnp.minimum(n, 1), n_row_blocks - 1), 0)


def _norm_chunk(h_ref, ssq_ref, g, xn_ref, chunk):
    rows = V7X_BF16_SUBLANES
    r = pl.multiple_of(chunk * rows, rows)
    x = h_ref[pl.ds(r, rows), :]
    ms = jnp.sum(ssq_ref[pl.ds(r, rows), :], axis=1, keepdims=True) * (1.0 / x.shape[1])
    xn_ref[pl.ds(r, rows), :] = (x * lax.rsqrt(ms + EPS) * g).astype(BF16)


def _norm_block(h_ref, ssq_ref, g_ref, xn_ref):
    g = g_ref[...]

    def body(i, c):
        _norm_chunk(h_ref, ssq_ref, g, xn_ref, i)
        return c

    lax.fori_loop(0, h_ref.shape[0] // V7X_BF16_SUBLANES, body, 0, unroll=_NORM_UNROLL)


def _norm_share(h_ref, ssq_ref, g_ref, xn_ref, step, n_steps):
    n_chunks = h_ref.shape[0] // V7X_BF16_SUBLANES
    per_step = -(-n_chunks // n_steps)
    g = g_ref[...]
    for u in range(per_step):
        _norm_chunk(h_ref, ssq_ref, g, xn_ref,
                    jnp.minimum(step * per_step + u, n_chunks - 1))


def _normed_matmul_steps(h_ref, ssq_ref, g_ref, xn_bufs, step_body):
    m = pl.program_id(0)
    n = pl.program_id(1)
    n_steps = pl.num_programs(1) - 1

    if len(xn_bufs) == 1:
        @pl.when(n == 0)
        def _():
            _norm_block(h_ref, ssq_ref, g_ref, xn_bufs[0])

        step_body(xn_bufs[0])
        return

    @pl.when(jnp.logical_and(m == 0, n == 0))
    def _():
        _norm_block(h_ref, ssq_ref, g_ref, xn_bufs[0])
        step_body(xn_bufs[0])

    for parity in (0, 1):
        mine, other = xn_bufs[parity], xn_bufs[1 - parity]

        @pl.when(jnp.logical_and(jnp.logical_and(m > 0, n == 0), m % 2 == parity))
        def _(mine=mine):
            step_body(mine)

        @pl.when(jnp.logical_and(n > 0, m % 2 == parity))
        def _(mine=mine, other=other):
            _norm_share(h_ref, ssq_ref, g_ref, other, n - 1, n_steps)
            step_body(mine)


def _row_ssq_kernel(h_ref, o_ref):
    x = h_ref[...]
    sq = x * x
    part = sq[:, 0:V7X_LANES]
    for t in range(1, x.shape[1] // V7X_LANES):
        part = part + sq[:, t * V7X_LANES:(t + 1) * V7X_LANES]
    o_ref[...] = part


def _row_ssq(h, tr):
    M, D = h.shape
    return pl.pallas_call(
        _row_ssq_kernel,
        grid=(M // tr,),
        in_specs=[pl.BlockSpec((tr, D), lambda m: (m, 0))],
        out_specs=pl.BlockSpec((tr, V7X_LANES), lambda m: (m, 0)),
        out_shape=jax.ShapeDtypeStruct((M, V7X_LANES), F32),
        compiler_params=_params(("arbitrary",)),
        name="row_ssq",
    )(h)


def _ffn_up_kernel(h_ref, ssq_ref, g_ref, wg_ref, wu_ref, o_ref, *xn_bufs):
    def step(xn_ref):
        xn = xn_ref[...]
        a = jnp.dot(xn, wg_ref[...], preferred_element_type=F32)
        b = jnp.dot(xn, wu_ref[...], preferred_element_type=F32)
        o_ref[...] = (a * _sigmoid(a) * b).astype(BF16)

    _normed_matmul_steps(h_ref, ssq_ref, g_ref, xn_bufs, step)


def _ffn_up(h, ssq, g, wg, wu, layer, tm, tf):
    M, D = h.shape
    F = wg.shape[2]
    fixed = 2 * tm * D * 4 + 2 * 2 * D * tf * 2 + 2 * tm * tf * 2 + 4 * tm * tf * 4
    lookahead = F // tf >= 2 and fixed + 2 * tm * D * 2 <= V7X_VMEM_LOOKAHEAD_BYTES
    rows = _lookahead_rows(M // tm) if lookahead else (lambda m, f: (m, 0))
    return pl.pallas_call(
        _ffn_up_kernel,
        grid=(M // tm, F // tf),
        in_specs=[pl.BlockSpec((tm, D), rows),
                  pl.BlockSpec((tm, V7X_LANES), rows),
                  pl.BlockSpec((1, D), lambda m, f: (0, 0)),
                  pl.BlockSpec((None, D, tf), lambda m, f: (layer, 0, f)),
                  pl.BlockSpec((None, D, tf), lambda m, f: (layer, 0, f))],
        out_specs=pl.BlockSpec((tm, tf), lambda m, f: (m, f)),
        out_shape=jax.ShapeDtypeStruct((M, F), BF16),
        scratch_shapes=[pltpu.VMEM((tm, D), BF16)] * (2 if lookahead else 1),
        compiler_params=_params(_ARB2),
        name="ffn_up",
    )(h, ssq, g, wg, wu)


def _mm_res_kernel(x_ref, w_ref, h_ref, o_ref, ssq_ref, *, scale):
    @pl.when(pl.program_id(1) == 0)
    def _():
        ssq_ref[...] = jnp.zeros(ssq_ref.shape, F32)

    acc = jnp.dot(x_ref[...], w_ref[...], preferred_element_type=F32)
    y = h_ref[...] + scale * acc
    o_ref[...] = y
    sq = y * y
    part = sq[:, 0:V7X_LANES]
    for t in range(1, y.shape[1] // V7X_LANES):
        part = part + sq[:, t * V7X_LANES:(t + 1) * V7X_LANES]
    ssq_ref[...] = ssq_ref[...] + part


def _mm_res(x, w, layer, h, scale, tm, tn):
    M, K = x.shape
    N = w.shape[2]
    return pl.pallas_call(
        functools.partial(_mm_res_kernel, scale=scale),
        grid=(M // tm, N // tn),
        in_specs=[pl.BlockSpec((tm, K), lambda m, n: (m, 0)),
                  pl.BlockSpec((None, K, tn), lambda m, n: (layer, 0, n)),
                  pl.BlockSpec((tm, tn), lambda m, n: (m, n))],
        out_specs=[pl.BlockSpec((tm, tn), lambda m, n: (m, n)),
                   pl.BlockSpec((tm, V7X_LANES), lambda m, n: (m, 0))],
        out_shape=[jax.ShapeDtypeStruct((M, N), F32),
                   jax.ShapeDtypeStruct((M, V7X_LANES), F32)],
        compiler_params=_params(_ARB2),
        name="mm_res",
    )(x, w, h)


def _proj_kernel(h_ref, ssq_ref, g_ref, w_ref, o_ref, xn_ref, *, n_q, n_lin, qscale):
    n = pl.program_id(1)

    @pl.when(n == 0)
    def _():
        _norm_block(h_ref, ssq_ref, g_ref, xn_ref)

    def acc():
        return jnp.dot(xn_ref[...], w_ref[...], preferred_element_type=F32)

    @pl.when(n < n_q)
    def _():
        o_ref[...] = (acc() * qscale).astype(BF16)

    @pl.when(jnp.logical_and(n >= n_q, n < n_lin))
    def _():
        o_ref[...] = acc().astype(BF16)

    @pl.when(n >= n_lin)
    def _():
        o_ref[...] = _sigmoid(acc()).astype(BF16)


def _proj(h, ssq, g, w, layer, tm, tn, q_width, gate_start, qscale):
    M, D = h.shape
    N = w.shape[2]
    assert q_width % tn == 0 and gate_start % tn == 0
    return pl.pallas_call(
        functools.partial(_proj_kernel, n_q=q_width // tn, n_lin=gate_start // tn,
                          qscale=qscale),
        grid=(M // tm, N // tn),
        in_specs=[pl.BlockSpec((tm, D), lambda m, n: (m, 0)),
                  pl.BlockSpec((tm, V7X_LANES), lambda m, n: (m, 0)),
                  pl.BlockSpec((1, D), lambda m, n: (0, 0)),
                  pl.BlockSpec((None, D, tn), lambda m, n: (layer, 0, n))],
        out_specs=pl.BlockSpec((tm, tn), lambda m, n: (m, n)),
        out_shape=jax.ShapeDtypeStruct((M, N), BF16),
        scratch_shapes=[pltpu.VMEM((tm, D), BF16)],
        compiler_params=_params(_ARB2),
        name="proj",
    )(h, ssq, g, w)


def _t5_bucket(rel, n_buckets):
    nb = n_buckets // 2
    max_exact = nb // 2
    ret = (rel > 0).astype(jnp.int32) * nb
    n = jnp.abs(rel)
    nf = jnp.maximum(n, 1).astype(jnp.float32)
    large = max_exact + (jnp.log(nf / max_exact) / math.log(MAX_DISTANCE / max_exact)
                         * (nb - max_exact)).astype(jnp.int32)
    large = jnp.minimum(large, nb - 1)
    return ret + jnp.where(n < max_exact, n, large)


_ATTN_TILE_ROWS = 432
_ATTN_UNROLL = 4
_ATTN_FAR_UNROLL = 16


class _AttnPlan:
    def __init__(self, L, tq_target, tk):
        self.L = L
        self.tk = tk
        self.tq = _divisor_tile(L, tq_target, V7X_BF16_SUBLANES)
        self.nq = L // self.tq
        self.nk = L // tk
        self.n_tail = L - self.nk * tk
        assert self.n_tail % V7X_BF16_SUBLANES == 0
        self.jl = [self.far_left(i * self.tq) for i in range(self.nq)]
        self.jr = [self.far_right(i * self.tq) for i in range(self.nq)]
        self.n_near = max(1, max(r - l for l, r in zip(self.jl, self.jr)))
        units = self.tq // V7X_BF16_SUBLANES
        n = max(1, round(self.tq / _ATTN_TILE_ROWS))
        sizes = [(units // n + (1 if t < units % n else 0)) * V7X_BF16_SUBLANES for t in range(n)]
        self.row_tiles = tuple((sum(sizes[:t]), sizes[t]) for t in range(n))

    def far_left(self, q0):
        return _near_range(q0, self.tq, self.tk, self.nk, min, max)[0]

    def far_right(self, q0):
        return _near_range(q0, self.tq, self.tk, self.nk, min, max)[1]


def _near_range(q0, tq, tk, nk, minimum, maximum):
    u = _ATTN_UNROLL
    lo = maximum(q0 - MAX_DISTANCE + 1, 0) // tk // u * u
    hi = (q0 + tq - 1 + MAX_DISTANCE + tk - 1) // tk
    hi = minimum((hi + u - 1) // u * u, nk)
    return minimum(lo, nk), hi


def _bias_vec_kernel(rb_ref, bk_ref, o_ref, *, n_buckets):
    bk = bk_ref[...]
    for h in range(o_ref.shape[0]):
        acc = jnp.zeros(bk.shape, F32)
        for b in range(n_buckets):
            acc = jnp.where(bk == b, rb_ref[b, h] * LOG2E, acc)
        o_ref[h:h + 1, :] = acc


def _bias_vectors(rel_bias, L, length):
    n_buckets, H = rel_bias.shape
    rel = jnp.arange(length, dtype=jnp.int32)[None, :] - (L - 1)
    return pl.pallas_call(
        functools.partial(_bias_vec_kernel, n_buckets=n_buckets),
        in_specs=[pl.BlockSpec(memory_space=pltpu.SMEM),
                  pl.BlockSpec((1, length), lambda: (0, 0))],
        out_specs=pl.BlockSpec((H, length), lambda: (0, 0)),
        out_shape=jax.ShapeDtypeStruct((H, length), F32),
        name="bias_vectors",
    )(rel_bias, _t5_bucket(rel, n_buckets))


_BANK_ROWS = 8
_BANK_UNROLL = 6


def _bank_kernel(rb_ref, bv_ref, o_ref, *, n_buckets, tq, tk, nk, n_near, n_tail, L):
    i = pl.program_id(0)
    t = pl.program_id(1)
    H = o_ref.shape[0]
    q0 = i * tq
    lo, _ = _near_range(q0, tq, tk, nk, jnp.minimum, jnp.maximum)
    is_chunk = t < n_near
    kstart = jnp.where(is_chunk, (lo + t) * tk, nk * tk)
    n_valid = jnp.where(is_chunk, tk, n_tail)
    all_left = jnp.logical_and(is_chunk, kstart + tk - 1 - q0 <= -MAX_DISTANCE)
    all_right = jnp.logical_and(is_chunk, kstart - (q0 + tq - 1) >= MAX_DISTANCE)

    def fill(bucket):
        for h in range(H):
            o_ref[h] = jnp.full((tq, tk), rb_ref[bucket, h] * LOG2E, F32)

    @pl.when(all_left)
    def _():
        fill(n_buckets // 2 - 1)

    @pl.when(all_right)
    def _():
        fill(n_buckets - 1)

    @pl.when(jnp.logical_not(jnp.logical_or(all_left, all_right)))
    def _():
        R = _BANK_ROWS
        win_w = tk + 2 * V7X_LANES
        col = lax.broadcasted_iota(jnp.int32, (R, tk), 1)

        def body(g, carry):
            r0 = g * R
            off = kstart - (q0 + r0 + R - 1) + (L - 1)
            off_al = pl.multiple_of(off // V7X_LANES * V7X_LANES, V7X_LANES)
            shift = (win_w - (R - 1) - (off - off_al)) % win_w
            for h in range(H):
                win = jnp.broadcast_to(bv_ref[h:h + 1, pl.ds(off_al, win_w)], (R, win_w))
                tile = pltpu.roll(win, shift, 1, stride=1, stride_axis=0)[:, :tk]
                o_ref[h, pl.ds(pl.multiple_of(r0, R), R), :] = jnp.where(col < n_valid, tile,
                                                                          -jnp.inf)
            return carry

        lax.fori_loop(0, tq // R, body, 0, unroll=_BANK_UNROLL)


def _bias_bank(rel_bias, plan):
    n_buckets, H = rel_bias.shape
    tq, tk, nq, nt = plan.tq, plan.tk, plan.nq, plan.n_near + 1
    length = -(-(plan.L + (plan.nk + nt) * tk + tk + 2 * V7X_LANES) // V7X_LANES) * V7X_LANES
    bvec = _bias_vectors(rel_bias, plan.L, length)
    return pl.pallas_call(
        functools.partial(_bank_kernel, n_buckets=n_buckets, tq=tq, tk=tk, nk=plan.nk,
                          n_near=plan.n_near, n_tail=plan.n_tail, L=plan.L),
        grid=(nq, nt),
        in_specs=[pl.BlockSpec(memory_space=pltpu.SMEM),
                  pl.BlockSpec((H, length), lambda i, t: (0, 0))],
        out_specs=pl.BlockSpec((H, None, None, tq, tk), lambda i, t: (0, i, t, 0, 0)),
        out_shape=jax.ShapeDtypeStruct((H, nq, nt, tq, tk), F32),
        compiler_params=_params(_ARB2),
        name="bias_bank",
    )(rel_bias, bvec)


def _attn_kernel(rb_ref, lam0_ref, q_ref, k_ref, v_ref, bank_ref, lq1_ref, lk1_ref, lq2_ref,
                 lk2_ref, sg_ref, o_ref, m1, l1, a1, m2, l2, a2, kt, vt, *,
                 tq, tk, nk, n_tail, n_near, dh, n_buckets, row_tiles):
    h = pl.program_id(0)
    i = pl.program_id(1)
    lam_init = lam0_ref[0]
    out_gain = lam0_ref[1]
    V = v_ref.shape[-1]
    q0 = i * tq
    jl, jr = _near_range(q0, tq, tk, nk, jnp.minimum, jnp.maximum)
    c_left = rb_ref[n_buckets // 2 - 1, h] * LOG2E
    c_right = rb_ref[n_buckets - 1, h] * LOG2E

    for m_ref, l_ref, a_ref in ((m1, l1, a1), (m2, l2, a2)):
        m_ref[...] = jnp.full(m_ref.shape, -jnp.inf, F32)
        l_ref[...] = jnp.zeros(l_ref.shape, F32)
        a_ref[...] = jnp.zeros(a_ref.shape, F32)

    maps = ((slice(0, dh), m1, l1, a1), (slice(dh, 2 * dh), m2, l2, a2))

    def update(rows, kmap, vc, bias, qcols, m_ref, l_ref, a_ref):
        s = lax.dot_general(q_ref[rows, qcols], kmap, (((1,), (1,)), ((), ())),
                            preferred_element_type=F32)
        if bias is not None:
            s = s + bias
        lane_tiles = s.shape[1] // V7X_LANES
        m_prev = m_ref[rows, :]
        m_next = jnp.maximum(m_prev, jnp.max(s, axis=1, keepdims=True))
        p = jnp.exp2(s - jnp.tile(m_next, (1, lane_tiles)))
        alpha = jnp.exp2(m_prev - m_next)
        psum = p[:, 0:V7X_LANES]
        for t in range(1, lane_tiles):
            psum = psum + p[:, t * V7X_LANES:(t + 1) * V7X_LANES]
        l_ref[rows, :] = alpha * l_ref[rows, :] + psum
        m_ref[rows, :] = m_next
        pv = jnp.dot(p.astype(BF16), vc, preferred_element_type=F32)
        a_ref[rows, :] = a_ref[rows, :] * jnp.tile(alpha, (1, V // V7X_LANES)) + pv

    def chunk(kv_rows, width, bias_tile, k_src, v_src):
        for r0, rq in row_tiles:
            rows = slice(r0, r0 + rq)
            for qcols, m_ref, l_ref, a_ref in maps:
                bias = None if bias_tile is None else bank_ref[bias_tile, rows, 0:width]
                update(rows, k_src[kv_rows, qcols], v_src[kv_rows, :], bias,
                       qcols, m_ref, l_ref, a_ref)

    def far_chunk(j):
        chunk(pl.ds(pl.multiple_of(j * tk, tk), tk), tk, None, k_ref, v_ref)

    def near_chunk(j):
        chunk(pl.ds(pl.multiple_of(j * tk, tk), tk), tk, j - jl, k_ref, v_ref)

    def sweep(lo, hi, one, unrolls):
        for unroll in unrolls:
            def group(t, carry, lo=lo, unroll=unroll):
                for u in range(unroll):
                    one(lo + unroll * t + u)
                return carry

            n_groups = (hi - lo) // unroll
            lax.fori_loop(0, n_groups, group, 0)
            lo = lo + unroll * n_groups

    def shift_max(delta):
        m1[...] = m1[...] + delta
        m2[...] = m2[...] + delta

    far_unrolls = (_ATTN_FAR_UNROLL, _ATTN_FAR_UNROLL // 2, _ATTN_UNROLL, 1)
    sweep(0, jl, far_chunk, far_unrolls)
    shift_max(c_left)
    sweep(jl, jr, near_chunk, (_ATTN_UNROLL, 1))
    shift_max(-c_right)
    sweep(jr, nk, far_chunk, far_unrolls)
    shift_max(c_right)

    if n_tail:
        kt[...] = jnp.zeros(kt.shape, BF16)
        vt[...] = jnp.zeros(vt.shape, BF16)
        kt[0:n_tail, :] = k_ref[nk * tk:nk * tk + n_tail, :]
        vt[0:n_tail, :] = v_ref[nk * tk:nk * tk + n_tail, :]
        chunk(slice(0, kt.shape[0]), kt.shape[0], n_near, kt, vt)

    lam = (jnp.exp(jnp.sum(lq1_ref[...] * lk1_ref[...], axis=-1, keepdims=True))
           - jnp.exp(jnp.sum(lq2_ref[...] * lk2_ref[...], axis=-1, keepdims=True))
           + lam_init)
    for r0, rq in row_tiles:
        rows = slice(r0, r0 + rq)
        w1 = 1.0 / jnp.sum(l1[rows, :], axis=1, keepdims=True)
        w2 = lam / jnp.sum(l2[rows, :], axis=1, keepdims=True)
        att = a1[rows, :] * w1 - a2[rows, :] * w2
        ms = jnp.mean(att * att, axis=-1, keepdims=True)
        y = att * lax.rsqrt(ms + EPS) * sg_ref[...] * out_gain
        o_ref[rows, :] = y.astype(BF16)


def _attention(proj3, bank, rel_bias, lq1, lk1, lq2, lk2, subln_g, plan, H, dh, V, lam_init):
    B, L, _ = proj3.shape
    assert V == 2 * dh and V % V7X_LANES == 0 and plan.tk % V7X_LANES == 0
    tq, tk = plan.tq, plan.tk
    nt = plan.n_near + 1
    tail_w = max(V7X_LANES, -(-plan.n_tail // V7X_LANES) * V7X_LANES)
    kern = functools.partial(
        _attn_kernel, tq=tq, tk=tk, nk=plan.nk, n_tail=plan.n_tail, n_near=plan.n_near,
        dh=dh, n_buckets=rel_bias.shape[0], row_tiles=plan.row_tiles)
    vec = lambda w: pl.BlockSpec((1, w), lambda h, i, b: (0, 0))
    lam0 = jnp.asarray([lam_init, 1.0 - lam_init], F32)
    return pl.pallas_call(
        kern,
        grid=(H, plan.nq, B),
        in_specs=[pl.BlockSpec(memory_space=pltpu.SMEM),
                  pl.BlockSpec(memory_space=pltpu.SMEM),
                  pl.BlockSpec((None, tq, V), lambda h, i, b: (b, i, h)),
                  pl.BlockSpec((None, L, V), lambda h, i, b: (b, 0, H + h)),
                  pl.BlockSpec((None, L, V), lambda h, i, b: (b, 0, 2 * H + h)),
                  pl.BlockSpec((None, None, nt, tq, tk), lambda h, i, b: (h, i, 0, 0, 0)),
                  vec(dh), vec(dh), vec(dh), vec(dh), vec(V)],
        out_specs=pl.BlockSpec((None, tq, V), lambda h, i, b: (b, i, h)),
        out_shape=jax.ShapeDtypeStruct((B, L, H * V), BF16),
        scratch_shapes=[pltpu.VMEM((tq, V7X_LANES), F32), pltpu.VMEM((tq, V7X_LANES), F32),
                        pltpu.VMEM((tq, V), F32),
                        pltpu.VMEM((tq, V7X_LANES), F32), pltpu.VMEM((tq, V7X_LANES), F32),
                        pltpu.VMEM((tq, V), F32),
                        pltpu.VMEM((tail_w, V), BF16), pltpu.VMEM((tail_w, V), BF16)],
        compiler_params=_params(_ARB3),
        name="attn",
    )(rel_bias, lam0, proj3, proj3, proj3, bank, lq1, lk1, lq2, lk2, subln_g)


_CONV_HALO = 16
_CONV_LANES = 256


def _conv_rows(tl):
    return _divisor_tile(tl, 48, V7X_BF16_SUBLANES)


def _conv_kernel(ua_ref, ub_ref, pa_ref, pb_ref, na_ref, nb_ref, w_ref, cb_ref, lg_ref, lb_ref,
                 o_ref, cpad, y_scr, *, tl, C, KW):
    i = pl.program_id(1)
    last = pl.num_programs(1) - 1
    H0 = _CONV_HALO
    RC = _conv_rows(tl)
    WIN = RC + 2 * H0
    off = H0 - KW // 2

    def glu(a, b):
        return a.astype(F32) * _sigmoid(b.astype(F32))

    cpad[0:H0, :] = jnp.where(i > 0, glu(pa_ref[...], pb_ref[...]), 0.0)
    cpad[H0 + tl:H0 + tl + H0, :] = jnp.where(i < last, glu(na_ref[...], nb_ref[...]), 0.0)

    def glu_body(c, carry):
        r = pl.multiple_of(c * RC, RC)
        cpad[pl.ds(H0 + r, RC), :] = glu(ua_ref[pl.ds(r, RC), :], ub_ref[pl.ds(r, RC), :])
        return carry

    lax.fori_loop(0, tl // RC, glu_body, 0)

    cg = min(C, _CONV_LANES)

    def conv_chunk(c):
        r = pl.multiple_of(c * RC, RC)
        for g in range(C // cg):
            cols = slice(g * cg, (g + 1) * cg)
            win = cpad[pl.ds(r, WIN), cols]
            acc = jnp.zeros((RC, cg), F32)
            for sh in range(8):
                taps = [t for t in range(KW) if (t + off) % 8 == sh]
                if not taps:
                    continue
                wsh = win if sh == 0 else pltpu.roll(win, WIN - sh, axis=0)
                for t in taps:
                    a = (t + off) // 8
                    acc = acc + wsh[8 * a:8 * a + RC, :] * w_ref[t:t + 1, cols]
            y_scr[pl.ds(r, RC), cols] = acc + cb_ref[:, cols]

    def ln_chunk(c):
        r = pl.multiple_of(c * RC, RC)
        x = y_scr[pl.ds(r, RC), :]
        mu = jnp.mean(x, axis=-1, keepdims=True)
        var = jnp.mean(jnp.square(x - mu), axis=-1, keepdims=True)
        y = (x - mu) * lax.rsqrt(var + LN_EPS) * lg_ref[...] + lb_ref[...]
        o_ref[pl.ds(r, RC), :] = (y * _sigmoid(y)).astype(BF16)

    conv_chunk(0)

    def body(c, carry):
        conv_chunk(c)
        ln_chunk(c - 1)
        return carry

    n_chunks = tl // RC
    lax.fori_loop(1, n_chunks, body, 0)
    ln_chunk(n_chunks - 1)


def _conv_module(proj3, conv_w, conv_b, ln_g, ln_b, u_start, tl):
    B, L, _ = proj3.shape
    KW, C = conv_w.shape
    H0 = _CONV_HALO
    assert KW // 2 <= H0 and KW - 1 + H0 - KW // 2 < 2 * H0 + 1
    assert u_start % C == 0 and L % tl == 0 and tl % V7X_BF16_SUBLANES == 0 and L % H0 == 0
    ca = u_start // C
    hb = tl // H0
    nhb = L // H0
    main = lambda col: pl.BlockSpec((None, tl, C), lambda b, i: (b, i, col))
    prev = lambda col: pl.BlockSpec((None, H0, C), lambda b, i: (b, jnp.maximum(i * hb - 1, 0), col))
    nxt = lambda col: pl.BlockSpec((None, H0, C), lambda b, i: (b, jnp.minimum((i + 1) * hb, nhb - 1), col))
    vec = pl.BlockSpec((1, C), lambda b, i: (0, 0))
    return pl.pallas_call(
        functools.partial(_conv_kernel, tl=tl, C=C, KW=KW),
        grid=(B, L // tl),
        in_specs=[main(ca), main(ca + 1), prev(ca), prev(ca + 1), nxt(ca), nxt(ca + 1),
                  pl.BlockSpec((KW, C), lambda b, i: (0, 0)), vec, vec, vec],
        out_specs=pl.BlockSpec((None, tl, C), lambda b, i: (b, i, 0)),
        out_shape=jax.ShapeDtypeStruct((B, L, C), BF16),
        scratch_shapes=[pltpu.VMEM((tl + 2 * H0, C), F32), pltpu.VMEM((tl, C), F32)],
        compiler_params=_params(_ARB2),
        name="conv",
    )(proj3, proj3, proj3, proj3, proj3, proj3, conv_w, conv_b, ln_g, ln_b)


def _merge_kernel(att_ref, cn_ref, wa_ref, wc_ref, ga_ref, gc_ref, o_ref):
    a = jnp.dot(att_ref[...], wa_ref[...], preferred_element_type=F32)
    c = jnp.dot(cn_ref[...], wc_ref[...], preferred_element_type=F32)
    o_ref[...] = (ga_ref[...].astype(F32) * a + gc_ref[...].astype(F32) * c).astype(BF16)


def _merge(att, cn, wa, wc, layer, proj, gate_start, tm, tn):
    M, A = att.shape
    C = cn.shape[1]
    D = wa.shape[2]
    assert gate_start % tn == 0 and D % tn == 0
    ga0 = gate_start // tn
    gc0 = (gate_start + D) // tn
    return pl.pallas_call(
        _merge_kernel,
        grid=(M // tm, D // tn),
        in_specs=[pl.BlockSpec((tm, A), lambda m, n: (m, 0)),
                  pl.BlockSpec((tm, C), lambda m, n: (m, 0)),
                  pl.BlockSpec((None, A, tn), lambda m, n: (layer, 0, n)),
                  pl.BlockSpec((None, C, tn), lambda m, n: (layer, 0, n)),
                  pl.BlockSpec((tm, tn), lambda m, n: (m, ga0 + n)),
                  pl.BlockSpec((tm, tn), lambda m, n: (m, gc0 + n))],
        out_specs=pl.BlockSpec((tm, tn), lambda m, n: (m, n)),
        out_shape=jax.ShapeDtypeStruct((M, D), BF16),
        compiler_params=_params(_ARB2),
        name="merge",
    )(att, cn, wa, wc, proj, proj)


def _final_norm_kernel(h_ref, g_ref, o_ref):
    x = h_ref[...]
    ms = jnp.mean(x * x, axis=-1, keepdims=True)
    o_ref[...] = x * lax.rsqrt(ms + EPS) * g_ref[...]


def _final_norm(h, g, B, L, n_meta, tr):
    D = h.shape[1]
    S = L - n_meta
    assert L % 8 == 0 and n_meta % 8 == 0 and tr % 8 == 0
    return pl.pallas_call(
        _final_norm_kernel,
        grid=(B, S // tr),
        in_specs=[pl.BlockSpec((pl.Element(tr), pl.Element(D)),
                               lambda b, i: (pl.multiple_of(b * L + n_meta + i * tr, 8), 0)),
                  pl.BlockSpec((1, D), lambda b, i: (0, 0))],
        out_specs=pl.BlockSpec((None, tr, D), lambda b, i: (b, i, 0)),
        out_shape=jax.ShapeDtypeStruct((B, S, D), F32),
        compiler_params=_params(_ARB2),
        name="final_norm",
    )(h, g)


def _conv_tile(L):
    for mult in (48, 32, V7X_BF16_SUBLANES):
        if L % mult == 0:
            return _divisor_tile(L, 2100, mult)
    raise ValueError(f"sequence length {L} is not a multiple of {V7X_BF16_SUBLANES}")


def _tiles(M, L, S, D, F, q_width, gate_start, d_in):
    sub = V7X_BF16_SUBLANES
    return dict(
        tm_norm=_divisor_tile(M, 704, sub),
        tm=_divisor_tile(M, 1024, sub),
        tf=_divisor_tile(F, 512, V7X_LANES),
        tn_res=_divisor_tile(D, 1024, V7X_LANES),
        tm_down=_divisor_tile(M, 2100, sub),
        tn_down=_divisor_tile(D, 512, V7X_LANES),
        tn_proj=_divisor_tile(math.gcd(q_width, gate_start, d_in), 1024, V7X_LANES),
        tn_merge=_divisor_tile(math.gcd(D, gate_start), 1024, V7X_LANES),
        tl=_conv_tile(L),
        tr=_divisor_tile(S, 512, 8),
        tq_target=928,
        tk=256,
    )


def _encode(x, meta, rel_bias, wts, final_norm, dims):
    H, dh, V = dims["H"], dims["dh"], dims["V"]
    B, S, D = x.shape
    n_meta = meta.shape[0]
    L = S + n_meta
    M = B * L
    depth = wts["w_in"].shape[0]
    F = wts["w_ffn1_gate"].shape[-1]
    KW, C = wts["conv_w"].shape[1:]
    d_in = wts["w_in"].shape[-1]
    qk_width = H * 2 * dh
    u_start = 2 * qk_width + H * V
    gate_start = u_start + 2 * C
    assert d_in == gate_start + 2 * D

    t = _tiles(M, L, S, D, F, qk_width, gate_start, d_in)
    plan = _AttnPlan(L, t["tq_target"], t["tk"])
    bank = _bias_bank(rel_bias, plan)
    qscale = dh ** -0.5 * LOG2E

    h = jnp.concatenate([jnp.broadcast_to(meta[None].astype(x.dtype), (B, n_meta, D)), x],
                        axis=1).reshape(M, D)
    ssq = _row_ssq(h, t["tm_norm"])
    row = lambda v: v.reshape(1, -1)
    for l in range(depth):
        lam_init = 0.8 - 0.6 * math.exp(-0.3 * l)
        hm = _ffn_up(h, ssq, row(wts["norm_ffn1"][l]), wts["w_ffn1_gate"], wts["w_ffn1_up"], l,
                     t["tm_norm"], t["tf"])
        h, ssq = _mm_res(hm, wts["w_ffn1_down"], l, h, 0.5, t["tm_down"], t["tn_down"])

        proj = _proj(h, ssq, row(wts["norm_mix"][l]), wts["w_in"], l, t["tm_norm"], t["tn_proj"],
                     qk_width, gate_start, qscale)
        proj3 = proj.reshape(B, L, d_in)
        att = _attention(proj3, bank, rel_bias, row(wts["lambda_q1"][l]), row(wts["lambda_k1"][l]),
                         row(wts["lambda_q2"][l]), row(wts["lambda_k2"][l]), row(wts["subln_g"][l]),
                         plan, H, dh, V, lam_init)
        cn = _conv_module(proj3, wts["conv_w"][l], row(wts["conv_b"][l]), row(wts["conv_ln_g"][l]),
                          row(wts["conv_ln_b"][l]), u_start, t["tl"])
        mg = _merge(att.reshape(M, H * V), cn.reshape(M, C), wts["w_attn_out"],
                    wts["w_conv_out"], l, proj, gate_start, t["tm"], t["tn_merge"])
        h, ssq = _mm_res(mg, wts["w_out"], l, h, 1.0, t["tm"], t["tn_res"])

        hm = _ffn_up(h, ssq, row(wts["norm_ffn2"][l]), wts["w_ffn2_gate"], wts["w_ffn2_up"], l,
                     t["tm_norm"], t["tf"])
        h, ssq = _mm_res(hm, wts["w_ffn2_down"], l, h, 0.5, t["tm_down"], t["tn_down"])
    return _final_norm(h, row(final_norm), B, L, n_meta, t["tr"])


_MATMUL_WEIGHTS = ("w_ffn1_gate", "w_ffn1_up", "w_ffn1_down", "w_in", "w_attn_out",
                   "w_conv_out", "w_out", "w_ffn2_gate", "w_ffn2_up", "w_ffn2_down")


def kernel(x_prompt, x_sample, meta_tokens, rel_bias, norm_ffn1, w_ffn1_gate, w_ffn1_up, w_ffn1_down, norm_mix, w_in, lambda_q1, lambda_k1, lambda_q2, lambda_k2, subln_g, w_attn_out, conv_w, conv_b, conv_ln_g, conv_ln_b, w_conv_out, w_out, norm_ffn2, w_ffn2_gate, w_ffn2_up, w_ffn2_down, final_norm):
    wts = dict(norm_ffn1=norm_ffn1, w_ffn1_gate=w_ffn1_gate, w_ffn1_up=w_ffn1_up,
               w_ffn1_down=w_ffn1_down, norm_mix=norm_mix, w_in=w_in, lambda_q1=lambda_q1,
               lambda_k1=lambda_k1, lambda_q2=lambda_q2, lambda_k2=lambda_k2, subln_g=subln_g,
               w_attn_out=w_attn_out, conv_w=conv_w, conv_b=conv_b, conv_ln_g=conv_ln_g,
               conv_ln_b=conv_ln_b, w_conv_out=w_conv_out, w_out=w_out, norm_ffn2=norm_ffn2,
               w_ffn2_gate=w_ffn2_gate, w_ffn2_up=w_ffn2_up, w_ffn2_down=w_ffn2_down)
    for name in _MATMUL_WEIGHTS:
        wts[name] = wts[name].astype(BF16)
    dh = lambda_q1.shape[-1]
    V = subln_g.shape[-1]
    dims = dict(H=rel_bias.shape[1], dh=dh, V=V)
    y_prompt = _encode(x_prompt, meta_tokens, rel_bias, wts, final_norm, dims)
    y_sample = _encode(x_sample, meta_tokens, rel_bias, wts, final_norm, dims)
    return (y_prompt, y_sample)
```

```python
import functools
import math

import jax
import jax.numpy as jnp
from jax import lax
from jax.experimental import pallas as pl
from jax.experimental.pallas import tpu as pltpu

F32 = jnp.float32
BF16 = jnp.bfloat16

EPS = 1e-6
LN_EPS = 1e-5
MAX_DISTANCE = 128
LOG2E = math.log2(math.e)

V7X_LANES = 128
V7X_BF16_SUBLANES = 16
V7X_VMEM_LIMIT_BYTES = 56 * 1024 * 1024
V7X_VMEM_LOOKAHEAD_BYTES = 52 * 1024 * 1024

_ARB2 = ("arbitrary", "arbitrary")
_ARB3 = ("arbitrary", "arbitrary", "arbitrary")


def _params(sem):
    return pltpu.CompilerParams(dimension_semantics=sem,
                                vmem_limit_bytes=V7X_VMEM_LIMIT_BYTES)


def _sigmoid(x):
    return 0.5 * jnp.tanh(0.5 * x) + 0.5


def _divisor_tile(n, target, mult):
    best = None
    for t in range(mult, min(n, target) + 1, mult):
        if n % t == 0:
            best = t
    if best is None:
        raise ValueError(f"no tile for n={n} target={target} mult={mult}")
    return best


_NORM_UNROLL = 4


def _lookahead_rows(n_row_blocks):
    return lambda m, n: (jnp.minimum(m + jnp.minimum(n, 1), n_row_blocks - 1), 0)


def _norm_chunk(h_ref, ssq_ref, g, xn_ref, chunk):
    rows = V7X_BF16_SUBLANES
    r = pl.multiple_of(chunk * rows, rows)
    x = h_ref[pl.ds(r, rows), :]
    ms = jnp.sum(ssq_ref[pl.ds(r, rows), :], axis=1, keepdims=True) * (1.0 / x.shape[1])
    xn_ref[pl.ds(r, rows), :] = (x * lax.rsqrt(ms + EPS) * g).astype(BF16)


def _norm_block(h_ref, ssq_ref, g_ref, xn_ref):
    g = g_ref[...]

    def body(i, c):
        _norm_chunk(h_ref, ssq_ref, g, xn_ref, i)
        return c

    lax.fori_loop(0, h_ref.shape[0] // V7X_BF16_SUBLANES, body, 0, unroll=_NORM_UNROLL)


def _norm_share(h_ref, ssq_ref, g_ref, xn_ref, step, n_steps):
    n_chunks = h_ref.shape[0] // V7X_BF16_SUBLANES
    per_step = -(-n_chunks // n_steps)
    g = g_ref[...]
    for u in range(per_step):
        _norm_chunk(h_ref, ssq_ref, g, xn_ref,
                    jnp.minimum(step * per_step + u, n_chunks - 1))


def _normed_matmul_steps(h_ref, ssq_ref, g_ref, xn_bufs, step_body):
    m = pl.program_id(0)
    n = pl.program_id(1)
    n_steps = pl.num_programs(1) - 1

    if len(xn_bufs) == 1:
        @pl.when(n == 0)
        def _():
            _norm_block(h_ref, ssq_ref, g_ref, xn_bufs[0])

        step_body(xn_bufs[0])
        return

    @pl.when(jnp.logical_and(m == 0, n == 0))
    def _():
        _norm_block(h_ref, ssq_ref, g_ref, xn_bufs[0])
        step_body(xn_bufs[0])

    for parity in (0, 1):
        mine, other = xn_bufs[parity], xn_bufs[1 - parity]

        @pl.when(jnp.logical_and(jnp.logical_and(m > 0, n == 0), m % 2 == parity))
        def _(mine=mine):
            step_body(mine)

        @pl.when(jnp.logical_and(n > 0, m % 2 == parity))
        def _(mine=mine, other=other):
            _norm_share(h_ref, ssq_ref, g_ref, other, n - 1, n_steps)
            step_body(mine)


def _row_ssq_kernel(h_ref, o_ref):
    x = h_ref[...]
    sq = x * x
    part = sq[:, 0:V7X_LANES]
    for t in range(1, x.shape[1] // V7X_LANES):
        part = part + sq[:, t * V7X_LANES:(t + 1) * V7X_LANES]
    o_ref[...] = part


def _row_ssq(h, tr):
    M, D = h.shape
    return pl.pallas_call(
        _row_ssq_kernel,
        grid=(M // tr,),
        in_specs=[pl.BlockSpec((tr, D), lambda m: (m, 0))],
        out_specs=pl.BlockSpec((tr, V7X_LANES), lambda m: (m, 0)),
        out_shape=jax.ShapeDtypeStruct((M, V7X_LANES), F32),
        compiler_params=_params(("arbitrary",)),
        name="row_ssq",
    )(h)


def _ffn_up_kernel(h_ref, ssq_ref, g_ref, wg_ref, wu_ref, o_ref, *xn_bufs):
    def step(xn_ref):
        xn = xn_ref[...]
        a = jnp.dot(xn, wg_ref[...], preferred_element_type=F32)
        b = jnp.dot(xn, wu_ref[...], preferred_element_type=F32)
        o_ref[...] = (a * _sigmoid(a) * b).astype(BF16)

    _normed_matmul_steps(h_ref, ssq_ref, g_ref, xn_bufs, step)


def _ffn_up(h, ssq, g, wg, wu, layer, tm, tf):
    M, D = h.shape
    F = wg.shape[2]
    fixed = 2 * tm * D * 4 + 2 * 2 * D * tf * 2 + 2 * tm * tf * 2 + 4 * tm * tf * 4
    lookahead = F // tf >= 2 and fixed + 2 * tm * D * 2 <= V7X_VMEM_LOOKAHEAD_BYTES
    rows = _lookahead_rows(M // tm) if lookahead else (lambda m, f: (m, 0))
    return pl.pallas_call(
        _ffn_up_kernel,
        grid=(M // tm, F // tf),
        in_specs=[pl.BlockSpec((tm, D), rows),
                  pl.BlockSpec((tm, V7X_LANES), rows),
                  pl.BlockSpec((1, D), lambda m, f: (0, 0)),
                  pl.BlockSpec((None, D, tf), lambda m, f: (layer, 0, f)),
                  pl.BlockSpec((None, D, tf), lambda m, f: (layer, 0, f))],
        out_specs=pl.BlockSpec((tm, tf), lambda m, f: (m, f)),
        out_shape=jax.ShapeDtypeStruct((M, F), BF16),
        scratch_shapes=[pltpu.VMEM((tm, D), BF16)] * (2 if lookahead else 1),
        compiler_params=_params(_ARB2),
        name="ffn_up",
    )(h, ssq, g, wg, wu)


def _mm_res_kernel(x_ref, w_ref, h_ref, o_ref, ssq_ref, *, scale):
    @pl.when(pl.program_id(1) == 0)
    def _():
        ssq_ref[...] = jnp.zeros(ssq_ref.shape, F32)

    acc = jnp.dot(x_ref[...], w_ref[...], preferred_element_type=F32)
    y = h_ref[...] + scale * acc
    o_ref[...] = y
    sq = y * y
    part = sq[:, 0:V7X_LANES]
    for t in range(1, y.shape[1] // V7X_LANES):
        part = part + sq[:, t * V7X_LANES:(t + 1) * V7X_LANES]
    ssq_ref[...] = ssq_ref[...] + part


def _mm_res(x, w, layer, h, scale, tm, tn):
    M, K = x.shape
    N = w.shape[2]
    return pl.pallas_call(
        functools.partial(_mm_res_kernel, scale=scale),
        grid=(M // tm, N // tn),
        in_specs=[pl.BlockSpec((tm, K), lambda m, n: (m, 0)),
                  pl.BlockSpec((None, K, tn), lambda m, n: (layer, 0, n)),
                  pl.BlockSpec((tm, tn), lambda m, n: (m, n))],
        out_specs=[pl.BlockSpec((tm, tn), lambda m, n: (m, n)),
                   pl.BlockSpec((tm, V7X_LANES), lambda m, n: (m, 0))],
        out_shape=[jax.ShapeDtypeStruct((M, N), F32),
                   jax.ShapeDtypeStruct((M, V7X_LANES), F32)],
        compiler_params=_params(_ARB2),
        name="mm_res",
    )(x, w, h)


def _proj_kernel(h_ref, ssq_ref, g_ref, w_ref, o_ref, xn_ref, *, n_q, n_lin, qscale):
    n = pl.program_id(1)

    @pl.when(n == 0)
    def _():
        _norm_block(h_ref, ssq_ref, g_ref, xn_ref)

    def acc():
        return jnp.dot(xn_ref[...], w_ref[...], preferred_element_type=F32)

    @pl.when(n < n_q)
    def _():
        o_ref[...] = (acc() * qscale).astype(BF16)

    @pl.when(jnp.logical_and(n >= n_q, n < n_lin))
    def _():
        o_ref[...] = acc().astype(BF16)

    @pl.when(n >= n_lin)
    def _():
        o_ref[...] = _sigmoid(acc()).astype(BF16)


def _proj(h, ssq, g, w, layer, tm, tn, q_width, gate_start, qscale):
    M, D = h.shape
    N = w.shape[2]
    assert q_width % tn == 0 and gate_start % tn == 0
    return pl.pallas_call(
        functools.partial(_proj_kernel, n_q=q_width // tn, n_lin=gate_start // tn,
                          qscale=qscale),
        grid=(M // tm, N // tn),
        in_specs=[pl.BlockSpec((tm, D), lambda m, n: (m, 0)),
                  pl.BlockSpec((tm, V7X_LANES), lambda m, n: (m, 0)),
                  pl.BlockSpec((1, D), lambda m, n: (0, 0)),
                  pl.BlockSpec((None, D, tn), lambda m, n: (layer, 0, n))],
        out_specs=pl.BlockSpec((tm, tn), lambda m, n: (m, n)),
        out_shape=jax.ShapeDtypeStruct((M, N), BF16),
        scratch_shapes=[pltpu.VMEM((tm, D), BF16)],
        compiler_params=_params(_ARB2),
        name="proj",
    )(h, ssq, g, w)


def _t5_bucket(rel, n_buckets):
    nb = n_buckets // 2
    max_exact = nb // 2
    ret = (rel > 0).astype(jnp.int32) * nb
    n = jnp.abs(rel)
    nf = jnp.maximum(n, 1).astype(jnp.float32)
    large = max_exact + (jnp.log(nf / max_exact) / math.log(MAX_DISTANCE / max_exact)
                         * (nb - max_exact)).astype(jnp.int32)
    large = jnp.minimum(large, nb - 1)
    return ret + jnp.where(n < max_exact, n, large)


_ATTN_TILE_ROWS = 432
_ATTN_UNROLL = 4
_ATTN_FAR_UNROLL = 16


class _AttnPlan:
    def __init__(self, L, tq_target, tk):
        self.L = L
        self.tk = tk
        self.tq = _divisor_tile(L, tq_target, V7X_BF16_SUBLANES)
        self.nq = L // self.tq
        self.nk = L // tk
        self.n_tail = L - self.nk * tk
        assert self.n_tail % V7X_BF16_SUBLANES == 0
        self.jl = [self.far_left(i * self.tq) for i in range(self.nq)]
        self.jr = [self.far_right(i * self.tq) for i in range(self.nq)]
        self.n_near = max(1, max(r - l for l, r in zip(self.jl, self.jr)))
        units = self.tq // V7X_BF16_SUBLANES
        n = max(1, round(self.tq / _ATTN_TILE_ROWS))
        sizes = [(units // n + (1 if t < units % n else 0)) * V7X_BF16_SUBLANES for t in range(n)]
        self.row_tiles = tuple((sum(sizes[:t]), sizes[t]) for t in range(n))

    def far_left(self, q0):
        return _near_range(q0, self.tq, self.tk, self.nk, min, max)[0]

    def far_right(self, q0):
        return _near_range(q0, self.tq, self.tk, self.nk, min, max)[1]


def _near_range(q0, tq, tk, nk, minimum, maximum):
    u = _ATTN_UNROLL
    lo = maximum(q0 - MAX_DISTANCE + 1, 0) // tk // u * u
    hi = (q0 + tq - 1 + MAX_DISTANCE + tk - 1) // tk
    hi = minimum((hi + u - 1) // u * u, nk)
    return minimum(lo, nk), hi


def _bias_vec_kernel(rb_ref, bk_ref, o_ref, *, n_buckets):
    bk = bk_ref[...]
    for h in range(o_ref.shape[0]):
        acc = jnp.zeros(bk.shape, F32)
        for b in range(n_buckets):
            acc = jnp.where(bk == b, rb_ref[b, h] * LOG2E, acc)
        o_ref[h:h + 1, :] = acc


def _bias_vectors(rel_bias, L, length):
    n_buckets, H = rel_bias.shape
    rel = jnp.arange(length, dtype=jnp.int32)[None, :] - (L - 1)
    return pl.pallas_call(
        functools.partial(_bias_vec_kernel, n_buckets=n_buckets),
        in_specs=[pl.BlockSpec(memory_space=pltpu.SMEM),
                  pl.BlockSpec((1, length), lambda: (0, 0))],
        out_specs=pl.BlockSpec((H, length), lambda: (0, 0)),
        out_shape=jax.ShapeDtypeStruct((H, length), F32),
        name="bias_vectors",
    )(rel_bias, _t5_bucket(rel, n_buckets))


_BANK_ROWS = 8
_BANK_UNROLL = 6


def _bank_kernel(rb_ref, bv_ref, o_ref, *, n_buckets, tq, tk, nk, n_near, n_tail, L):
    i = pl.program_id(0)
    t = pl.program_id(1)
    H = o_ref.shape[0]
    q0 = i * tq
    lo, _ = _near_range(q0, tq, tk, nk, jnp.minimum, jnp.maximum)
    is_chunk = t < n_near
    kstart = jnp.where(is_chunk, (lo + t) * tk, nk * tk)
    n_valid = jnp.where(is_chunk, tk, n_tail)
    all_left = jnp.logical_and(is_chunk, kstart + tk - 1 - q0 <= -MAX_DISTANCE)
    all_right = jnp.logical_and(is_chunk, kstart - (q0 + tq - 1) >= MAX_DISTANCE)

    def fill(bucket):
        for h in range(H):
            o_ref[h] = jnp.full((tq, tk), rb_ref[bucket, h] * LOG2E, F32)

    @pl.when(all_left)
    def _():
        fill(n_buckets // 2 - 1)

    @pl.when(all_right)
    def _():
        fill(n_buckets - 1)

    @pl.when(jnp.logical_not(jnp.logical_or(all_left, all_right)))
    def _():
        R = _BANK_ROWS
        win_w = tk + 2 * V7X_LANES
        col = lax.broadcasted_iota(jnp.int32, (R, tk), 1)

        def body(g, carry):
            r0 = g * R
            off = kstart - (q0 + r0 + R - 1) + (L - 1)
            off_al = pl.multiple_of(off // V7X_LANES * V7X_LANES, V7X_LANES)
            shift = (win_w - (R - 1) - (off - off_al)) % win_w
            for h in range(H):
                win = jnp.broadcast_to(bv_ref[h:h + 1, pl.ds(off_al, win_w)], (R, win_w))
                tile = pltpu.roll(win, shift, 1, stride=1, stride_axis=0)[:, :tk]
                o_ref[h, pl.ds(pl.multiple_of(r0, R), R), :] = jnp.where(col < n_valid, tile,
                                                                          -jnp.inf)
            return carry

        lax.fori_loop(0, tq // R, body, 0, unroll=_BANK_UNROLL)


def _bias_bank(rel_bias, plan):
    n_buckets, H = rel_bias.shape
    tq, tk, nq, nt = plan.tq, plan.tk, plan.nq, plan.n_near + 1
    length = -(-(plan.L + (plan.nk + nt) * tk + tk + 2 * V7X_LANES) // V7X_LANES) * V7X_LANES
    bvec = _bias_vectors(rel_bias, plan.L, length)
    return pl.pallas_call(
        functools.partial(_bank_kernel, n_buckets=n_buckets, tq=tq, tk=tk, nk=plan.nk,
                          n_near=plan.n_near, n_tail=plan.n_tail, L=plan.L),
        grid=(nq, nt),
        in_specs=[pl.BlockSpec(memory_space=pltpu.SMEM),
                  pl.BlockSpec((H, length), lambda i, t: (0, 0))],
        out_specs=pl.BlockSpec((H, None, None, tq, tk), lambda i, t: (0, i, t, 0, 0)),
        out_shape=jax.ShapeDtypeStruct((H, nq, nt, tq, tk), F32),
        compiler_params=_params(_ARB2),
        name="bias_bank",
    )(rel_bias, bvec)


def _attn_kernel(rb_ref, lam0_ref, q_ref, k_ref, v_ref, bank_ref, lq1_ref, lk1_ref, lq2_ref,
                 lk2_ref, sg_ref, o_ref, m1, l1, a1, m2, l2, a2, kt, vt, *,
                 tq, tk, nk, n_tail, n_near, dh, n_buckets, row_tiles):
    h = pl.program_id(0)
    i = pl.program_id(1)
    lam_init = lam0_ref[0]
    out_gain = lam0_ref[1]
    V = v_ref.shape[-1]
    q0 = i * tq
    jl, jr = _near_range(q0, tq, tk, nk, jnp.minimum, jnp.maximum)
    c_left = rb_ref[n_buckets // 2 - 1, h] * LOG2E
    c_right = rb_ref[n_buckets - 1, h] * LOG2E

    for m_ref, l_ref, a_ref in ((m1, l1, a1), (m2, l2, a2)):
        m_ref[...] = jnp.full(m_ref.shape, -jnp.inf, F32)
        l_ref[...] = jnp.zeros(l_ref.shape, F32)
        a_ref[...] = jnp.zeros(a_ref.shape, F32)

    maps = ((slice(0, dh), m1, l1, a1), (slice(dh, 2 * dh), m2, l2, a2))

    def update(rows, kmap, vc, bias, qcols, m_ref, l_ref, a_ref):
        s = lax.dot_general(q_ref[rows, qcols], kmap, (((1,), (1,)), ((), ())),
                            preferred_element_type=F32)
        if bias is not None:
            s = s + bias
        lane_tiles = s.shape[1] // V7X_LANES
        m_prev = m_ref[rows, :]
        m_next = jnp.maximum(m_prev, jnp.max(s, axis=1, keepdims=True))
        p = jnp.exp2(s - jnp.tile(m_next, (1, lane_tiles)))
        alpha = jnp.exp2(m_prev - m_next)
        psum = p[:, 0:V7X_LANES]
        for t in range(1, lane_tiles):
            psum = psum + p[:, t * V7X_LANES:(t + 1) * V7X_LANES]
        l_ref[rows, :] = alpha * l_ref[rows, :] + psum
        m_ref[rows, :] = m_next
        pv = jnp.dot(p.astype(BF16), vc, preferred_element_type=F32)
        a_ref[rows, :] = a_ref[rows, :] * jnp.tile(alpha, (1, V // V7X_LANES)) + pv

    def chunk(kv_rows, width, bias_tile, k_src, v_src):
        for r0, rq in row_tiles:
            rows = slice(r0, r0 + rq)
            for qcols, m_ref, l_ref, a_ref in maps:
                bias = None if bias_tile is None else bank_ref[bias_tile, rows, 0:width]
                update(rows, k_src[kv_rows, qcols], v_src[kv_rows, :], bias,
                       qcols, m_ref, l_ref, a_ref)

    def far_chunk(j):
        chunk(pl.ds(pl.multiple_of(j * tk, tk), tk), tk, None, k_ref, v_ref)

    def near_chunk(j):
        chunk(pl.ds(pl.multiple_of(j * tk, tk), tk), tk, j - jl, k_ref, v_ref)

    def sweep(lo, hi, one, unrolls):
        for unroll in unrolls:
            def group(t, carry, lo=lo, unroll=unroll):
                for u in range(unroll):
                    one(lo + unroll * t + u)
                return carry

            n_groups = (hi - lo) // unroll
            lax.fori_loop(0, n_groups, group, 0)
            lo = lo + unroll * n_groups

    def shift_max(delta):
        m1[...] = m1[...] + delta
        m2[...] = m2[...] + delta

    far_unrolls = (_ATTN_FAR_UNROLL, _ATTN_FAR_UNROLL // 2, _ATTN_UNROLL, 1)
    sweep(0, jl, far_chunk, far_unrolls)
    shift_max(c_left)
    sweep(jl, jr, near_chunk, (2 * _ATTN_UNROLL, _ATTN_UNROLL, 1))
    shift_max(-c_right)
    sweep(jr, nk, far_chunk, far_unrolls)
    shift_max(c_right)

    if n_tail:
        kt[...] = jnp.zeros(kt.shape, BF16)
        vt[...] = jnp.zeros(vt.shape, BF16)
        kt[0:n_tail, :] = k_ref[nk * tk:nk * tk + n_tail, :]
        vt[0:n_tail, :] = v_ref[nk * tk:nk * tk + n_tail, :]
        chunk(slice(0, kt.shape[0]), kt.shape[0], n_near, kt, vt)

    lam = (jnp.exp(jnp.sum(lq1_ref[...] * lk1_ref[...], axis=-1, keepdims=True))
           - jnp.exp(jnp.sum(lq2_ref[...] * lk2_ref[...], axis=-1, keepdims=True))
           + lam_init)
    for r0, rq in row_tiles:
        rows = slice(r0, r0 + rq)
        w1 = 1.0 / jnp.sum(l1[rows, :], axis=1, keepdims=True)
        w2 = lam / jnp.sum(l2[rows, :], axis=1, keepdims=True)
        att = a1[rows, :] * w1 - a2[rows, :] * w2
        ms = jnp.mean(att * att, axis=-1, keepdims=True)
        y = att * lax.rsqrt(ms + EPS) * sg_ref[...] * out_gain
        o_ref[rows, :] = y.astype(BF16)


def _attention(proj3, bank, rel_bias, lq1, lk1, lq2, lk2, subln_g, plan, H, dh, V, lam_init):
    B, L, _ = proj3.shape
    assert V == 2 * dh and V % V7X_LANES == 0 and plan.tk % V7X_LANES == 0
    tq, tk = plan.tq, plan.tk
    nt = plan.n_near + 1
    tail_w = max(V7X_LANES, -(-plan.n_tail // V7X_LANES) * V7X_LANES)
    kern = functools.partial(
        _attn_kernel, tq=tq, tk=tk, nk=plan.nk, n_tail=plan.n_tail, n_near=plan.n_near,
        dh=dh, n_buckets=rel_bias.shape[0], row_tiles=plan.row_tiles)
    vec = lambda w: pl.BlockSpec((1, w), lambda h, i, b: (0, 0))
    lam0 = jnp.asarray([lam_init, 1.0 - lam_init], F32)
    return pl.pallas_call(
        kern,
        grid=(H, plan.nq, B),
        in_specs=[pl.BlockSpec(memory_space=pltpu.SMEM),
                  pl.BlockSpec(memory_space=pltpu.SMEM),
                  pl.BlockSpec((None, tq, V), lambda h, i, b: (b, i, h)),
                  pl.BlockSpec((None, L, V), lambda h, i, b: (b, 0, H + h)),
                  pl.BlockSpec((None, L, V), lambda h, i, b: (b, 0, 2 * H + h)),
                  pl.BlockSpec((None, None, nt, tq, tk), lambda h, i, b: (h, i, 0, 0, 0)),
                  vec(dh), vec(dh), vec(dh), vec(dh), vec(V)],
        out_specs=pl.BlockSpec((None, tq, V), lambda h, i, b: (b, i, h)),
        out_shape=jax.ShapeDtypeStruct((B, L, H * V), BF16),
        scratch_shapes=[pltpu.VMEM((tq, V7X_LANES), F32), pltpu.VMEM((tq, V7X_LANES), F32),
                        pltpu.VMEM((tq, V), F32),
                        pltpu.VMEM((tq, V7X_LANES), F32), pltpu.VMEM((tq, V7X_LANES), F32),
                        pltpu.VMEM((tq, V), F32),
                        pltpu.VMEM((tail_w, V), BF16), pltpu.VMEM((tail_w, V), BF16)],
        compiler_params=_params(_ARB3),
        name="attn",
    )(rel_bias, lam0, proj3, proj3, proj3, bank, lq1, lk1, lq2, lk2, subln_g)


_CONV_HALO = 16
_CONV_LANES = 256


def _conv_rows(tl):
    return _divisor_tile(tl, 48, V7X_BF16_SUBLANES)


def _conv_kernel(ua_ref, ub_ref, pa_ref, pb_ref, na_ref, nb_ref, w_ref, cb_ref, lg_ref, lb_ref,
                 o_ref, cpad, y_scr, *, tl, C, KW):
    i = pl.program_id(1)
    last = pl.num_programs(1) - 1
    H0 = _CONV_HALO
    RC = _conv_rows(tl)
    WIN = RC + 2 * H0
    off = H0 - KW // 2

    def glu(a, b):
        return a.astype(F32) * _sigmoid(b.astype(F32))

    cpad[0:H0, :] = jnp.where(i > 0, glu(pa_ref[...], pb_ref[...]), 0.0)
    cpad[H0 + tl:H0 + tl + H0, :] = jnp.where(i < last, glu(na_ref[...], nb_ref[...]), 0.0)

    def glu_body(c, carry):
        r = pl.multiple_of(c * RC, RC)
        cpad[pl.ds(H0 + r, RC), :] = glu(ua_ref[pl.ds(r, RC), :], ub_ref[pl.ds(r, RC), :])
        return carry

    lax.fori_loop(0, tl // RC, glu_body, 0)

    cg = min(C, _CONV_LANES)

    def conv_chunk(c):
        r = pl.multiple_of(c * RC, RC)
        for g in range(C // cg):
            cols = slice(g * cg, (g + 1) * cg)
            win = cpad[pl.ds(r, WIN), cols]
            acc = jnp.zeros((RC, cg), F32)
            for sh in range(8):
                taps = [t for t in range(KW) if (t + off) % 8 == sh]
                if not taps:
                    continue
                wsh = win if sh == 0 else pltpu.roll(win, WIN - sh, axis=0)
                for t in taps:
                    a = (t + off) // 8
                    acc = acc + wsh[8 * a:8 * a + RC, :] * w_ref[t:t + 1, cols]
            y_scr[pl.ds(r, RC), cols] = acc + cb_ref[:, cols]

    def ln_chunk(c):
        r = pl.multiple_of(c * RC, RC)
        x = y_scr[pl.ds(r, RC), :]
        mu = jnp.mean(x, axis=-1, keepdims=True)
        var = jnp.mean(jnp.square(x - mu), axis=-1, keepdims=True)
        y = (x - mu) * lax.rsqrt(var + LN_EPS) * lg_ref[...] + lb_ref[...]
        o_ref[pl.ds(r, RC), :] = (y * _sigmoid(y)).astype(BF16)

    conv_chunk(0)

    def body(c, carry):
        conv_chunk(c)
        ln_chunk(c - 1)
        return carry

    n_chunks = tl // RC
    lax.fori_loop(1, n_chunks, body, 0)
    ln_chunk(n_chunks - 1)


def _conv_module(proj3, conv_w, conv_b, ln_g, ln_b, u_start, tl):
    B, L, _ = proj3.shape
    KW, C = conv_w.shape
    H0 = _CONV_HALO
    assert KW // 2 <= H0 and KW - 1 + H0 - KW // 2 < 2 * H0 + 1
    assert u_start % C == 0 and L % tl == 0 and tl % V7X_BF16_SUBLANES == 0 and L % H0 == 0
    ca = u_start // C
    hb = tl // H0
    nhb = L // H0
    main = lambda col: pl.BlockSpec((None, tl, C), lambda b, i: (b, i, col))
    prev = lambda col: pl.BlockSpec((None, H0, C), lambda b, i: (b, jnp.maximum(i * hb - 1, 0), col))
    nxt = lambda col: pl.BlockSpec((None, H0, C), lambda b, i: (b, jnp.minimum((i + 1) * hb, nhb - 1), col))
    vec = pl.BlockSpec((1, C), lambda b, i: (0, 0))
    return pl.pallas_call(
        functools.partial(_conv_kernel, tl=tl, C=C, KW=KW),
        grid=(B, L // tl),
        in_specs=[main(ca), main(ca + 1), prev(ca), prev(ca + 1), nxt(ca), nxt(ca + 1),
                  pl.BlockSpec((KW, C), lambda b, i: (0, 0)), vec, vec, vec],
        out_specs=pl.BlockSpec((None, tl, C), lambda b, i: (b, i, 0)),
        out_shape=jax.ShapeDtypeStruct((B, L, C), BF16),
        scratch_shapes=[pltpu.VMEM((tl + 2 * H0, C), F32), pltpu.VMEM((tl, C), F32)],
        compiler_params=_params(_ARB2),
        name="conv",
    )(proj3, proj3, proj3, proj3, proj3, proj3, conv_w, conv_b, ln_g, ln_b)


def _merge_kernel(att_ref, cn_ref, wa_ref, wc_ref, ga_ref, gc_ref, o_ref):
    a = jnp.dot(att_ref[...], wa_ref[...], preferred_element_type=F32)
    c = jnp.dot(cn_ref[...], wc_ref[...], preferred_element_type=F32)
    o_ref[...] = (ga_ref[...].astype(F32) * a + gc_ref[...].astype(F32) * c).astype(BF16)


def _merge(att, cn, wa, wc, layer, proj, gate_start, tm, tn):
    M, A = att.shape
    C = cn.shape[1]
    D = wa.shape[2]
    assert gate_start % tn == 0 and D % tn == 0
    ga0 = gate_start // tn
    gc0 = (gate_start + D) // tn
    return pl.pallas_call(
        _merge_kernel,
        grid=(M // tm, D // tn),
        in_specs=[pl.BlockSpec((tm, A), lambda m, n: (m, 0)),
                  pl.BlockSpec((tm, C), lambda m, n: (m, 0)),
                  pl.BlockSpec((None, A, tn), lambda m, n: (layer, 0, n)),
                  pl.BlockSpec((None, C, tn), lambda m, n: (layer, 0, n)),
                  pl.BlockSpec((tm, tn), lambda m, n: (m, ga0 + n)),
                  pl.BlockSpec((tm, tn), lambda m, n: (m, gc0 + n))],
        out_specs=pl.BlockSpec((tm, tn), lambda m, n: (m, n)),
        out_shape=jax.ShapeDtypeStruct((M, D), BF16),
        compiler_params=_params(_ARB2),
        name="merge",
    )(att, cn, wa, wc, proj, proj)


def _final_norm_kernel(h_ref, g_ref, o_ref):
    x = h_ref[...]
    ms = jnp.mean(x * x, axis=-1, keepdims=True)
    o_ref[...] = x * lax.rsqrt(ms + EPS) * g_ref[...]


def _final_norm(h, g, B, L, n_meta, tr):
    D = h.shape[1]
    S = L - n_meta
    assert L % 8 == 0 and n_meta % 8 == 0 and tr % 8 == 0
    return pl.pallas_call(
        _final_norm_kernel,
        grid=(B, S // tr),
        in_specs=[pl.BlockSpec((pl.Element(tr), pl.Element(D)),
                               lambda b, i: (pl.multiple_of(b * L + n_meta + i * tr, 8), 0)),
                  pl.BlockSpec((1, D), lambda b, i: (0, 0))],
        out_specs=pl.BlockSpec((None, tr, D), lambda b, i: (b, i, 0)),
        out_shape=jax.ShapeDtypeStruct((B, S, D), F32),
        compiler_params=_params(_ARB2),
        name="final_norm",
    )(h, g)


def _conv_tile(L):
    for mult in (48, 32, V7X_BF16_SUBLANES):
        if L % mult == 0:
            return _divisor_tile(L, 2100, mult)
    raise ValueError(f"sequence length {L} is not a multiple of {V7X_BF16_SUBLANES}")


def _tiles(M, L, S, D, F, q_width, gate_start, d_in):
    sub = V7X_BF16_SUBLANES
    return dict(
        tm_norm=_divisor_tile(M, 704, sub),
        tm=_divisor_tile(M, 1024, sub),
        tf=_divisor_tile(F, 512, V7X_LANES),
        tn_res=_divisor_tile(D, 1024, V7X_LANES),
        tm_down=_divisor_tile(M, 2100, sub),
        tn_down=_divisor_tile(D, 512, V7X_LANES),
        tn_proj=_divisor_tile(math.gcd(q_width, gate_start, d_in), 1024, V7X_LANES),
        tn_merge=_divisor_tile(math.gcd(D, gate_start), 1024, V7X_LANES),
        tl=_conv_tile(L),
        tr=_divisor_tile(S, 512, 8),
        tq_target=704 if L <= 4096 else 448,
        tk=256,
    )


def _encode(x, meta, rel_bias, wts, final_norm, dims):
    H, dh, V = dims["H"], dims["dh"], dims["V"]
    B, S, D = x.shape
    n_meta = meta.shape[0]
    L = S + n_meta
    M = B * L
    depth = wts["w_in"].shape[0]
    F = wts["w_ffn1_gate"].shape[-1]
    KW, C = wts["conv_w"].shape[1:]
    d_in = wts["w_in"].shape[-1]
    qk_width = H * 2 * dh
    u_start = 2 * qk_width + H * V
    gate_start = u_start + 2 * C
    assert d_in == gate_start + 2 * D

    t = _tiles(M, L, S, D, F, qk_width, gate_start, d_in)
    plan = _AttnPlan(L, t["tq_target"], t["tk"])
    bank = _bias_bank(rel_bias, plan)
    qscale = dh ** -0.5 * LOG2E

    h = jnp.concatenate([jnp.broadcast_to(meta[None].astype(x.dtype), (B, n_meta, D)), x],
                        axis=1).reshape(M, D)
    ssq = _row_ssq(h, t["tm_norm"])
    row = lambda v: v.reshape(1, -1)
    for l in range(depth):
        lam_init = 0.8 - 0.6 * math.exp(-0.3 * l)
        hm = _ffn_up(h, ssq, row(wts["norm_ffn1"][l]), wts["w_ffn1_gate"], wts["w_ffn1_up"], l,
                     t["tm_norm"], t["tf"])
        h, ssq = _mm_res(hm, wts["w_ffn1_down"], l, h, 0.5, t["tm_down"], t["tn_down"])

        proj = _proj(h, ssq, row(wts["norm_mix"][l]), wts["w_in"], l, t["tm_norm"], t["tn_proj"],
                     qk_width, gate_start, qscale)
        proj3 = proj.reshape(B, L, d_in)
        att = _attention(proj3, bank, rel_bias, row(wts["lambda_q1"][l]), row(wts["lambda_k1"][l]),
                         row(wts["lambda_q2"][l]), row(wts["lambda_k2"][l]), row(wts["subln_g"][l]),
                         plan, H, dh, V, lam_init)
        cn = _conv_module(proj3, wts["conv_w"][l], row(wts["conv_b"][l]), row(wts["conv_ln_g"][l]),
                          row(wts["conv_ln_b"][l]), u_start, t["tl"])
        mg = _merge(att.reshape(M, H * V), cn.reshape(M, C), wts["w_attn_out"],
                    wts["w_conv_out"], l, proj, gate_start, t["tm"], t["tn_merge"])
        h, ssq = _mm_res(mg, wts["w_out"], l, h, 1.0, t["tm"], t["tn_res"])

        hm = _ffn_up(h, ssq, row(wts["norm_ffn2"][l]), wts["w_ffn2_gate"], wts["w_ffn2_up"], l,
                     t["tm_norm"], t["tf"])
        h, ssq = _mm_res(hm, wts["w_ffn2_down"], l, h, 0.5, t["tm_down"], t["tn_down"])
    return _final_norm(h, row(final_norm), B, L, n_meta, t["tr"])


_MATMUL_WEIGHTS = ("w_ffn1_gate", "w_ffn1_up", "w_ffn1_down", "w_in", "w_attn_out",
                   "w_conv_out", "w_out", "w_ffn2_gate", "w_ffn2_up", "w_ffn2_down")


def kernel(x_prompt, x_sample, meta_tokens, rel_bias, norm_ffn1, w_ffn1_gate, w_ffn1_up, w_ffn1_down, norm_mix, w_in, lambda_q1, lambda_k1, lambda_q2, lambda_k2, subln_g, w_attn_out, conv_w, conv_b, conv_ln_g, conv_ln_b, w_conv_out, w_out, norm_ffn2, w_ffn2_gate, w_ffn2_up, w_ffn2_down, final_norm):
    wts = dict(norm_ffn1=norm_ffn1, w_ffn1_gate=w_ffn1_gate, w_ffn1_up=w_ffn1_up,
               w_ffn1_down=w_ffn1_down, norm_mix=norm_mix, w_in=w_in, lambda_q1=lambda_q1,
               lambda_k1=lambda_k1, lambda_q2=lambda_q2, lambda_k2=lambda_k2, subln_g=subln_g,
               w_attn_out=w_attn_out, conv_w=conv_w, conv_b=conv_b, conv_ln_g=conv_ln_g,
               conv_ln_b=conv_ln_b, w_conv_out=w_conv_out, w_out=w_out, norm_ffn2=norm_ffn2,
               w_ffn2_gate=w_ffn2_gate, w_ffn2_up=w_ffn2_up, w_ffn2_down=w_ffn2_down)
    for name in _MATMUL_WEIGHTS:
        wts[name] = wts[name].astype(BF16)
    dh = lambda_q1.shape[-1]
    V = subln_g.shape[-1]
    dims = dict(H=rel_bias.shape[1], dh=dh, V=V)
    y_prompt = _encode(x_prompt, meta_tokens, rel_bias, wts, final_norm, dims)
    y_sample = _encode(x_sample, meta_tokens, rel_bias, wts, final_norm, dims)
    return (y_prompt, y_sample)
```

```python
import functools
import math

import jax
import jax.numpy as jnp
from jax import lax
from jax.experimental import pallas as pl
from jax.experimental.pallas import tpu as pltpu

F32 = jnp.float32
BF16 = jnp.bfloat16

EPS = 1e-6
LN_EPS = 1e-5
MAX_DISTANCE = 128
LOG2E = math.log2(math.e)

V7X_LANES = 128
V7X_BF16_SUBLANES = 16
V7X_VMEM_LIMIT_BYTES = 58 * 1024 * 1024
V7X_VMEM_LOOKAHEAD_BYTES = 56 * 1024 * 1024

_ARB2 = ("arbitrary", "arbitrary")
_ARB3 = ("arbitrary", "arbitrary", "arbitrary")


def _params(sem):
    return pltpu.CompilerParams(dimension_semantics=sem,
                                vmem_limit_bytes=V7X_VMEM_LIMIT_BYTES)


def _sigmoid(x):
    return 0.5 * jnp.tanh(0.5 * x) + 0.5


def _divisor_tile(n, target, mult):
    best = None
    for t in range(mult, min(n, target) + 1, mult):
        if n % t == 0:
            best = t
    if best is None:
        raise ValueError(f"no tile for n={n} target={target} mult={mult}")
    return best


_NORM_UNROLL = 4


def _lookahead_rows(n_row_blocks):
    return lambda m, n: (jnp.minimum(m + jnp.minimum(n, 1), n_row_blocks - 1), 0)


def _norm_chunk(h_ref, ssq_ref, g, xn_ref, chunk):
    rows = V7X_BF16_SUBLANES
    r = pl.multiple_of(chunk * rows, rows)
    x = h_ref[pl.ds(r, rows), :]
    ms = jnp.sum(ssq_ref[pl.ds(r, rows), :], axis=1, keepdims=True) * (1.0 / x.shape[1])
    xn_ref[pl.ds(r, rows), :] = (x * lax.rsqrt(ms + EPS) * g).astype(BF16)


def _norm_block(h_ref, ssq_ref, g_ref, xn_ref):
    g = g_ref[...]

    def body(i, c):
        _norm_chunk(h_ref, ssq_ref, g, xn_ref, i)
        return c

    lax.fori_loop(0, h_ref.shape[0] // V7X_BF16_SUBLANES, body, 0, unroll=_NORM_UNROLL)


def _norm_share(h_ref, ssq_ref, g_ref, xn_ref, step, n_steps):
    n_chunks = h_ref.shape[0] // V7X_BF16_SUBLANES
    per_step = -(-n_chunks // n_steps)
    g = g_ref[...]
    for u in range(per_step):
        _norm_chunk(h_ref, ssq_ref, g, xn_ref,
                    jnp.minimum(step * per_step + u, n_chunks - 1))


def _normed_matmul_steps(h_ref, ssq_ref, g_ref, xn_bufs, step_body):
    m = pl.program_id(0)
    n = pl.program_id(1)
    n_steps = pl.num_programs(1) - 1

    if len(xn_bufs) == 1:
        @pl.when(n == 0)
        def _():
            _norm_block(h_ref, ssq_ref, g_ref, xn_bufs[0])

        step_body(xn_bufs[0])
        return

    @pl.when(jnp.logical_and(m == 0, n == 0))
    def _():
        _norm_block(h_ref, ssq_ref, g_ref, xn_bufs[0])
        step_body(xn_bufs[0])

    for parity in (0, 1):
        mine, other = xn_bufs[parity], xn_bufs[1 - parity]

        @pl.when(jnp.logical_and(jnp.logical_and(m > 0, n == 0), m % 2 == parity))
        def _(mine=mine):
            step_body(mine)

        @pl.when(jnp.logical_and(n > 0, m % 2 == parity))
        def _(mine=mine, other=other):
            _norm_share(h_ref, ssq_ref, g_ref, other, n - 1, n_steps)
            step_body(mine)


def _row_ssq_kernel(h_ref, o_ref):
    x = h_ref[...]
    sq = x * x
    part = sq[:, 0:V7X_LANES]
    for t in range(1, x.shape[1] // V7X_LANES):
        part = part + sq[:, t * V7X_LANES:(t + 1) * V7X_LANES]
    o_ref[...] = part


def _row_ssq(h, tr):
    M, D = h.shape
    return pl.pallas_call(
        _row_ssq_kernel,
        grid=(M // tr,),
        in_specs=[pl.BlockSpec((tr, D), lambda m: (m, 0))],
        out_specs=pl.BlockSpec((tr, V7X_LANES), lambda m: (m, 0)),
        out_shape=jax.ShapeDtypeStruct((M, V7X_LANES), F32),
        compiler_params=_params(("arbitrary",)),
        name="row_ssq",
    )(h)


def _ffn_up_kernel(h_ref, ssq_ref, g_ref, wg_ref, wu_ref, o_ref, *xn_bufs):
    def step(xn_ref):
        xn = xn_ref[...]
        a = jnp.dot(xn, wg_ref[...], preferred_element_type=F32)
        b = jnp.dot(xn, wu_ref[...], preferred_element_type=F32)
        o_ref[...] = (a * _sigmoid(a) * b).astype(BF16)

    _normed_matmul_steps(h_ref, ssq_ref, g_ref, xn_bufs, step)


def _ffn_up(h, ssq, g, wg, wu, layer, tm, tf):
    M, D = h.shape
    F = wg.shape[2]
    fixed = 2 * tm * D * 4 + 2 * 2 * D * tf * 2 + 2 * tm * tf * 2 + 4 * tm * tf * 4
    lookahead = F // tf >= 2 and fixed + 2 * tm * D * 2 <= V7X_VMEM_LOOKAHEAD_BYTES
    rows = _lookahead_rows(M // tm) if lookahead else (lambda m, f: (m, 0))
    return pl.pallas_call(
        _ffn_up_kernel,
        grid=(M // tm, F // tf),
        in_specs=[pl.BlockSpec((tm, D), rows),
                  pl.BlockSpec((tm, V7X_LANES), rows),
                  pl.BlockSpec((1, D), lambda m, f: (0, 0)),
                  pl.BlockSpec((None, D, tf), lambda m, f: (layer, 0, f)),
                  pl.BlockSpec((None, D, tf), lambda m, f: (layer, 0, f))],
        out_specs=pl.BlockSpec((tm, tf), lambda m, f: (m, f)),
        out_shape=jax.ShapeDtypeStruct((M, F), BF16),
        scratch_shapes=[pltpu.VMEM((tm, D), BF16)] * (2 if lookahead else 1),
        compiler_params=_params(_ARB2),
        name="ffn_up",
    )(h, ssq, g, wg, wu)


def _mm_res_kernel(x_ref, w_ref, h_ref, o_ref, ssq_ref, *, scale):
    @pl.when(pl.program_id(1) == 0)
    def _():
        ssq_ref[...] = jnp.zeros(ssq_ref.shape, F32)

    acc = jnp.dot(x_ref[...], w_ref[...], preferred_element_type=F32)
    y = h_ref[...] + scale * acc
    o_ref[...] = y
    sq = y * y
    part = sq[:, 0:V7X_LANES]
    for t in range(1, y.shape[1] // V7X_LANES):
        part = part + sq[:, t * V7X_LANES:(t + 1) * V7X_LANES]
    ssq_ref[...] = ssq_ref[...] + part


def _mm_res(x, w, layer, h, scale, tm, tn):
    M, K = x.shape
    N = w.shape[2]
    return pl.pallas_call(
        functools.partial(_mm_res_kernel, scale=scale),
        grid=(M // tm, N // tn),
        in_specs=[pl.BlockSpec((tm, K), lambda m, n: (m, 0)),
                  pl.BlockSpec((None, K, tn), lambda m, n: (layer, 0, n)),
                  pl.BlockSpec((tm, tn), lambda m, n: (m, n))],
        out_specs=[pl.BlockSpec((tm, tn), lambda m, n: (m, n)),
                   pl.BlockSpec((tm, V7X_LANES), lambda m, n: (m, 0))],
        out_shape=[jax.ShapeDtypeStruct((M, N), F32),
                   jax.ShapeDtypeStruct((M, V7X_LANES), F32)],
        compiler_params=_params(_ARB2),
        name="mm_res",
    )(x, w, h)


def _proj_kernel(h_ref, ssq_ref, g_ref, w_ref, o_ref, xn_ref, *, n_q, n_lin, qscale):
    n = pl.program_id(1)

    @pl.when(n == 0)
    def _():
        _norm_block(h_ref, ssq_ref, g_ref, xn_ref)

    def acc():
        return jnp.dot(xn_ref[...], w_ref[...], preferred_element_type=F32)

    @pl.when(n < n_q)
    def _():
        o_ref[...] = (acc() * qscale).astype(BF16)

    @pl.when(jnp.logical_and(n >= n_q, n < n_lin))
    def _():
        o_ref[...] = acc().astype(BF16)

    @pl.when(n >= n_lin)
    def _():
        o_ref[...] = _sigmoid(acc()).astype(BF16)


def _proj(h, ssq, g, w, layer, tm, tn, q_width, gate_start, qscale):
    M, D = h.shape
    N = w.shape[2]
    assert q_width % tn == 0 and gate_start % tn == 0
    return pl.pallas_call(
        functools.partial(_proj_kernel, n_q=q_width // tn, n_lin=gate_start // tn,
                          qscale=qscale),
        grid=(M // tm, N // tn),
        in_specs=[pl.BlockSpec((tm, D), lambda m, n: (m, 0)),
                  pl.BlockSpec((tm, V7X_LANES), lambda m, n: (m, 0)),
                  pl.BlockSpec((1, D), lambda m, n: (0, 0)),
                  pl.BlockSpec((None, D, tn), lambda m, n: (layer, 0, n))],
        out_specs=pl.BlockSpec((tm, tn), lambda m, n: (m, n)),
        out_shape=jax.ShapeDtypeStruct((M, N), BF16),
        scratch_shapes=[pltpu.VMEM((tm, D), BF16)],
        compiler_params=_params(_ARB2),
        name="proj",
    )(h, ssq, g, w)


def _t5_bucket(rel, n_buckets):
    nb = n_buckets // 2
    max_exact = nb // 2
    ret = (rel > 0).astype(jnp.int32) * nb
    n = jnp.abs(rel)
    nf = jnp.maximum(n, 1).astype(jnp.float32)
    large = max_exact + (jnp.log(nf / max_exact) / math.log(MAX_DISTANCE / max_exact)
                         * (nb - max_exact)).astype(jnp.int32)
    large = jnp.minimum(large, nb - 1)
    return ret + jnp.where(n < max_exact, n, large)


_ATTN_TILE_ROWS = 432
_ATTN_UNROLL = 4
_ATTN_FAR_UNROLL = 16


class _AttnPlan:
    def __init__(self, L, tq_target, tk):
        self.L = L
        self.tk = tk
        self.tq = _divisor_tile(L, tq_target, V7X_BF16_SUBLANES)
        self.nq = L // self.tq
        self.nk = L // tk
        self.n_tail = L - self.nk * tk
        assert self.n_tail % V7X_BF16_SUBLANES == 0
        self.jl = [self.far_left(i * self.tq) for i in range(self.nq)]
        self.jr = [self.far_right(i * self.tq) for i in range(self.nq)]
        self.n_near = max(1, max(r - l for l, r in zip(self.jl, self.jr)))
        units = self.tq // V7X_BF16_SUBLANES
        n = max(1, round(self.tq / _ATTN_TILE_ROWS))
        sizes = [(units // n + (1 if t < units % n else 0)) * V7X_BF16_SUBLANES for t in range(n)]
        self.row_tiles = tuple((sum(sizes[:t]), sizes[t]) for t in range(n))

    def far_left(self, q0):
        return _near_range(q0, self.tq, self.tk, self.nk, min, max)[0]

    def far_right(self, q0):
        return _near_range(q0, self.tq, self.tk, self.nk, min, max)[1]


def _near_range(q0, tq, tk, nk, minimum, maximum):
    u = _ATTN_UNROLL
    lo = maximum(q0 - MAX_DISTANCE + 1, 0) // tk // u * u
    hi = (q0 + tq - 1 + MAX_DISTANCE + tk - 1) // tk
    hi = minimum((hi + u - 1) // u * u, nk)
    return minimum(lo, nk), hi


def _bias_vec_kernel(rb_ref, bk_ref, o_ref, *, n_buckets):
    bk = bk_ref[...]
    for h in range(o_ref.shape[0]):
        acc = jnp.zeros(bk.shape, F32)
        for b in range(n_buckets):
            acc = jnp.where(bk == b, rb_ref[b, h] * LOG2E, acc)
        o_ref[h:h + 1, :] = acc


def _bias_vectors(rel_bias, L, length):
    n_buckets, H = rel_bias.shape
    rel = jnp.arange(length, dtype=jnp.int32)[None, :] - (L - 1)
    return pl.pallas_call(
        functools.partial(_bias_vec_kernel, n_buckets=n_buckets),
        in_specs=[pl.BlockSpec(memory_space=pltpu.SMEM),
                  pl.BlockSpec((1, length), lambda: (0, 0))],
        out_specs=pl.BlockSpec((H, length), lambda: (0, 0)),
        out_shape=jax.ShapeDtypeStruct((H, length), F32),
        name="bias_vectors",
    )(rel_bias, _t5_bucket(rel, n_buckets))


_BANK_ROWS = 8
_BANK_UNROLL = 6


def _bank_kernel(rb_ref, bv_ref, o_ref, *, n_buckets, tq, tk, nk, n_near, n_tail, L):
    i = pl.program_id(0)
    t = pl.program_id(1)
    H = o_ref.shape[0]
    q0 = i * tq
    lo, _ = _near_range(q0, tq, tk, nk, jnp.minimum, jnp.maximum)
    is_chunk = t < n_near
    kstart = jnp.where(is_chunk, (lo + t) * tk, nk * tk)
    n_valid = jnp.where(is_chunk, tk, n_tail)
    all_left = jnp.logical_and(is_chunk, kstart + tk - 1 - q0 <= -MAX_DISTANCE)
    all_right = jnp.logical_and(is_chunk, kstart - (q0 + tq - 1) >= MAX_DISTANCE)

    def fill(bucket):
        for h in range(H):
            o_ref[h] = jnp.full((tq, tk), rb_ref[bucket, h] * LOG2E, F32)

    @pl.when(all_left)
    def _():
        fill(n_buckets // 2 - 1)

    @pl.when(all_right)
    def _():
        fill(n_buckets - 1)

    @pl.when(jnp.logical_not(jnp.logical_or(all_left, all_right)))
    def _():
        R = _BANK_ROWS
        win_w = tk + 2 * V7X_LANES
        col = lax.broadcasted_iota(jnp.int32, (R, tk), 1)

        def body(g, carry):
            r0 = g * R
            off = kstart - (q0 + r0 + R - 1) + (L - 1)
            off_al = pl.multiple_of(off // V7X_LANES * V7X_LANES, V7X_LANES)
            shift = (win_w - (R - 1) - (off - off_al)) % win_w
            for h in range(H):
                win = jnp.broadcast_to(bv_ref[h:h + 1, pl.ds(off_al, win_w)], (R, win_w))
                tile = pltpu.roll(win, shift, 1, stride=1, stride_axis=0)[:, :tk]
                o_ref[h, pl.ds(pl.multiple_of(r0, R), R), :] = jnp.where(col < n_valid, tile,
                                                                          -jnp.inf)
            return carry

        lax.fori_loop(0, tq // R, body, 0, unroll=_BANK_UNROLL)


def _bias_bank(rel_bias, plan):
    n_buckets, H = rel_bias.shape
    tq, tk, nq, nt = plan.tq, plan.tk, plan.nq, plan.n_near + 1
    length = -(-(plan.L + (plan.nk + nt) * tk + tk + 2 * V7X_LANES) // V7X_LANES) * V7X_LANES
    bvec = _bias_vectors(rel_bias, plan.L, length)
    return pl.pallas_call(
        functools.partial(_bank_kernel, n_buckets=n_buckets, tq=tq, tk=tk, nk=plan.nk,
                          n_near=plan.n_near, n_tail=plan.n_tail, L=plan.L),
        grid=(nq, nt),
        in_specs=[pl.BlockSpec(memory_space=pltpu.SMEM),
                  pl.BlockSpec((H, length), lambda i, t: (0, 0))],
        out_specs=pl.BlockSpec((H, None, None, tq, tk), lambda i, t: (0, i, t, 0, 0)),
        out_shape=jax.ShapeDtypeStruct((H, nq, nt, tq, tk), F32),
        compiler_params=_params(_ARB2),
        name="bias_bank",
    )(rel_bias, bvec)


def _attn_kernel(rb_ref, lam0_ref, q_ref, k_ref, v_ref, bank_ref, lq1_ref, lk1_ref, lq2_ref,
                 lk2_ref, sg_ref, o_ref, m1, l1, a1, m2, l2, a2, kt, vt, *,
                 tq, tk, nk, n_tail, n_near, dh, n_buckets, row_tiles):
    h = pl.program_id(0)
    i = pl.program_id(1)
    lam_init = lam0_ref[0]
    out_gain = lam0_ref[1]
    V = v_ref.shape[-1]
    q0 = i * tq
    jl, jr = _near_range(q0, tq, tk, nk, jnp.minimum, jnp.maximum)
    c_left = rb_ref[n_buckets // 2 - 1, h] * LOG2E
    c_right = rb_ref[n_buckets - 1, h] * LOG2E

    for m_ref, l_ref, a_ref in ((m1, l1, a1), (m2, l2, a2)):
        m_ref[...] = jnp.full(m_ref.shape, -jnp.inf, F32)
        l_ref[...] = jnp.zeros(l_ref.shape, F32)
        a_ref[...] = jnp.zeros(a_ref.shape, F32)

    maps = ((slice(0, dh), m1, l1, a1), (slice(dh, 2 * dh), m2, l2, a2))

    def update(rows, kmap, vc, bias, qcols, m_ref, l_ref, a_ref):
        s = lax.dot_general(q_ref[rows, qcols], kmap, (((1,), (1,)), ((), ())),
                            preferred_element_type=F32)
        if bias is not None:
            s = s + bias
        lane_tiles = s.shape[1] // V7X_LANES
        m_prev = m_ref[rows, :]
        m_next = jnp.maximum(m_prev, jnp.max(s, axis=1, keepdims=True))
        p = jnp.exp2(s - jnp.tile(m_next, (1, lane_tiles)))
        alpha = jnp.exp2(m_prev - m_next)
        psum = p[:, 0:V7X_LANES]
        for t in range(1, lane_tiles):
            psum = psum + p[:, t * V7X_LANES:(t + 1) * V7X_LANES]
        l_ref[rows, :] = alpha * l_ref[rows, :] + psum
        m_ref[rows, :] = m_next
        pv = jnp.dot(p.astype(BF16), vc, preferred_element_type=F32)
        a_ref[rows, :] = a_ref[rows, :] * jnp.tile(alpha, (1, V // V7X_LANES)) + pv

    def chunk(kv_rows, width, bias_tile, k_src, v_src):
        for r0, rq in row_tiles:
            rows = slice(r0, r0 + rq)
            for qcols, m_ref, l_ref, a_ref in maps:
                bias = None if bias_tile is None else bank_ref[bias_tile, rows, 0:width]
                update(rows, k_src[kv_rows, qcols], v_src[kv_rows, :], bias,
                       qcols, m_ref, l_ref, a_ref)

    def far_chunk(j):
        chunk(pl.ds(pl.multiple_of(j * tk, tk), tk), tk, None, k_ref, v_ref)

    def near_chunk(j):
        chunk(pl.ds(pl.multiple_of(j * tk, tk), tk), tk, j - jl, k_ref, v_ref)

    def sweep(lo, hi, one, unrolls):
        for unroll in unrolls:
            def group(t, carry, lo=lo, unroll=unroll):
                for u in range(unroll):
                    one(lo + unroll * t + u)
                return carry

            n_groups = (hi - lo) // unroll
            lax.fori_loop(0, n_groups, group, 0)
            lo = lo + unroll * n_groups

    def shift_max(delta):
        m1[...] = m1[...] + delta
        m2[...] = m2[...] + delta

    far_unrolls = (_ATTN_FAR_UNROLL, _ATTN_FAR_UNROLL // 2, _ATTN_UNROLL, 1)
    sweep(0, jl, far_chunk, far_unrolls)
    shift_max(c_left)
    sweep(jl, jr, near_chunk, (2 * _ATTN_UNROLL, _ATTN_UNROLL, 1))
    shift_max(-c_right)
    sweep(jr, nk, far_chunk, far_unrolls)
    shift_max(c_right)

    if n_tail:
        kt[...] = jnp.zeros(kt.shape, BF16)
        vt[...] = jnp.zeros(vt.shape, BF16)
        kt[0:n_tail, :] = k_ref[nk * tk:nk * tk + n_tail, :]
        vt[0:n_tail, :] = v_ref[nk * tk:nk * tk + n_tail, :]
        chunk(slice(0, kt.shape[0]), kt.shape[0], n_near, kt, vt)

    lam = (jnp.exp(jnp.sum(lq1_ref[...] * lk1_ref[...], axis=-1, keepdims=True))
           - jnp.exp(jnp.sum(lq2_ref[...] * lk2_ref[...], axis=-1, keepdims=True))
           + lam_init)
    for r0, rq in row_tiles:
        rows = slice(r0, r0 + rq)
        w1 = 1.0 / jnp.sum(l1[rows, :], axis=1, keepdims=True)
        w2 = lam / jnp.sum(l2[rows, :], axis=1, keepdims=True)
        att = a1[rows, :] * w1 - a2[rows, :] * w2
        ms = jnp.mean(att * att, axis=-1, keepdims=True)
        y = att * lax.rsqrt(ms + EPS) * sg_ref[...] * out_gain
        o_ref[rows, :] = y.astype(BF16)


def _attention(proj3, bank, rel_bias, lq1, lk1, lq2, lk2, subln_g, plan, H, dh, V, lam_init):
    B, L, _ = proj3.shape
    assert V == 2 * dh and V % V7X_LANES == 0 and plan.tk % V7X_LANES == 0
    tq, tk = plan.tq, plan.tk
    nt = plan.n_near + 1
    tail_w = max(V7X_LANES, -(-plan.n_tail // V7X_LANES) * V7X_LANES)
    kern = functools.partial(
        _attn_kernel, tq=tq, tk=tk, nk=plan.nk, n_tail=plan.n_tail, n_near=plan.n_near,
        dh=dh, n_buckets=rel_bias.shape[0], row_tiles=plan.row_tiles)
    vec = lambda w: pl.BlockSpec((1, w), lambda h, i, b: (0, 0))
    lam0 = jnp.asarray([lam_init, 1.0 - lam_init], F32)
    return pl.pallas_call(
        kern,
        grid=(H, plan.nq, B),
        in_specs=[pl.BlockSpec(memory_space=pltpu.SMEM),
                  pl.BlockSpec(memory_space=pltpu.SMEM),
                  pl.BlockSpec((None, tq, V), lambda h, i, b: (b, i, h)),
                  pl.BlockSpec((None, L, V), lambda h, i, b: (b, 0, H + h)),
                  pl.BlockSpec((None, L, V), lambda h, i, b: (b, 0, 2 * H + h)),
                  pl.BlockSpec((None, None, nt, tq, tk), lambda h, i, b: (h, i, 0, 0, 0)),
                  vec(dh), vec(dh), vec(dh), vec(dh), vec(V)],
        out_specs=pl.BlockSpec((None, tq, V), lambda h, i, b: (b, i, h)),
        out_shape=jax.ShapeDtypeStruct((B, L, H * V), BF16),
        scratch_shapes=[pltpu.VMEM((tq, V7X_LANES), F32), pltpu.VMEM((tq, V7X_LANES), F32),
                        pltpu.VMEM((tq, V), F32),
                        pltpu.VMEM((tq, V7X_LANES), F32), pltpu.VMEM((tq, V7X_LANES), F32),
                        pltpu.VMEM((tq, V), F32),
                        pltpu.VMEM((tail_w, V), BF16), pltpu.VMEM((tail_w, V), BF16)],
        compiler_params=_params(_ARB3),
        name="attn",
    )(rel_bias, lam0, proj3, proj3, proj3, bank, lq1, lk1, lq2, lk2, subln_g)


_CONV_HALO = 16
_CONV_LANES = 256


def _conv_rows(tl):
    return _divisor_tile(tl, 48, V7X_BF16_SUBLANES)


def _conv_kernel(ua_ref, ub_ref, pa_ref, pb_ref, na_ref, nb_ref, w_ref, cb_ref, lg_ref, lb_ref,
                 o_ref, cpad, y_scr, *, tl, C, KW):
    i = pl.program_id(1)
    last = pl.num_programs(1) - 1
    H0 = _CONV_HALO
    RC = _conv_rows(tl)
    WIN = RC + 2 * H0
    off = H0 - KW // 2

    def glu(a, b):
        return a.astype(F32) * _sigmoid(b.astype(F32))

    cpad[0:H0, :] = jnp.where(i > 0, glu(pa_ref[...], pb_ref[...]), 0.0)
    cpad[H0 + tl:H0 + tl + H0, :] = jnp.where(i < last, glu(na_ref[...], nb_ref[...]), 0.0)

    def glu_body(c, carry):
        r = pl.multiple_of(c * RC, RC)
        cpad[pl.ds(H0 + r, RC), :] = glu(ua_ref[pl.ds(r, RC), :], ub_ref[pl.ds(r, RC), :])
        return carry

    lax.fori_loop(0, tl // RC, glu_body, 0)

    cg = min(C, _CONV_LANES)

    def conv_chunk(c):
        r = pl.multiple_of(c * RC, RC)
        for g in range(C // cg):
            cols = slice(g * cg, (g + 1) * cg)
            win = cpad[pl.ds(r, WIN), cols]
            acc = jnp.zeros((RC, cg), F32)
            for sh in range(8):
                taps = [t for t in range(KW) if (t + off) % 8 == sh]
                if not taps:
                    continue
                wsh = win if sh == 0 else pltpu.roll(win, WIN - sh, axis=0)
                for t in taps:
                    a = (t + off) // 8
                    acc = acc + wsh[8 * a:8 * a + RC, :] * w_ref[t:t + 1, cols]
            y_scr[pl.ds(r, RC), cols] = acc + cb_ref[:, cols]

    def ln_chunk(c):
        r = pl.multiple_of(c * RC, RC)
        x = y_scr[pl.ds(r, RC), :]
        mu = jnp.mean(x, axis=-1, keepdims=True)
        var = jnp.mean(jnp.square(x - mu), axis=-1, keepdims=True)
        y = (x - mu) * lax.rsqrt(var + LN_EPS) * lg_ref[...] + lb_ref[...]
        o_ref[pl.ds(r, RC), :] = (y * _sigmoid(y)).astype(BF16)

    conv_chunk(0)

    def body(c, carry):
        conv_chunk(c)
        ln_chunk(c - 1)
        return carry

    n_chunks = tl // RC
    lax.fori_loop(1, n_chunks, body, 0)
    ln_chunk(n_chunks - 1)


def _conv_module(proj3, conv_w, conv_b, ln_g, ln_b, u_start, tl):
    B, L, _ = proj3.shape
    KW, C = conv_w.shape
    H0 = _CONV_HALO
    assert KW // 2 <= H0 and KW - 1 + H0 - KW // 2 < 2 * H0 + 1
    assert u_start % C == 0 and L % tl == 0 and tl % V7X_BF16_SUBLANES == 0 and L % H0 == 0
    ca = u_start // C
    hb = tl // H0
    nhb = L // H0
    main = lambda col: pl.BlockSpec((None, tl, C), lambda b, i: (b, i, col))
    prev = lambda col: pl.BlockSpec((None, H0, C), lambda b, i: (b, jnp.maximum(i * hb - 1, 0), col))
    nxt = lambda col: pl.BlockSpec((None, H0, C), lambda b, i: (b, jnp.minimum((i + 1) * hb, nhb - 1), col))
    vec = pl.BlockSpec((1, C), lambda b, i: (0, 0))
    return pl.pallas_call(
        functools.partial(_conv_kernel, tl=tl, C=C, KW=KW),
        grid=(B, L // tl),
        in_specs=[main(ca), main(ca + 1), prev(ca), prev(ca + 1), nxt(ca), nxt(ca + 1),
                  pl.BlockSpec((KW, C), lambda b, i: (0, 0)), vec, vec, vec],
        out_specs=pl.BlockSpec((None, tl, C), lambda b, i: (b, i, 0)),
        out_shape=jax.ShapeDtypeStruct((B, L, C), BF16),
        scratch_shapes=[pltpu.VMEM((tl + 2 * H0, C), F32), pltpu.VMEM((tl, C), F32)],
        compiler_params=_params(_ARB2),
        name="conv",
    )(proj3, proj3, proj3, proj3, proj3, proj3, conv_w, conv_b, ln_g, ln_b)


def _merge_kernel(att_ref, cn_ref, wa_ref, wc_ref, ga_ref, gc_ref, o_ref):
    a = jnp.dot(att_ref[...], wa_ref[...], preferred_element_type=F32)
    c = jnp.dot(cn_ref[...], wc_ref[...], preferred_element_type=F32)
    o_ref[...] = (ga_ref[...].astype(F32) * a + gc_ref[...].astype(F32) * c).astype(BF16)


def _merge(att, cn, wa, wc, layer, proj, gate_start, tm, tn):
    M, A = att.shape
    C = cn.shape[1]
    D = wa.shape[2]
    assert gate_start % tn == 0 and D % tn == 0
    ga0 = gate_start // tn
    gc0 = (gate_start + D) // tn
    return pl.pallas_call(
        _merge_kernel,
        grid=(M // tm, D // tn),
        in_specs=[pl.BlockSpec((tm, A), lambda m, n: (m, 0)),
                  pl.BlockSpec((tm, C), lambda m, n: (m, 0)),
                  pl.BlockSpec((None, A, tn), lambda m, n: (layer, 0, n)),
                  pl.BlockSpec((None, C, tn), lambda m, n: (layer, 0, n)),
                  pl.BlockSpec((tm, tn), lambda m, n: (m, ga0 + n)),
                  pl.BlockSpec((tm, tn), lambda m, n: (m, gc0 + n))],
        out_specs=pl.BlockSpec((tm, tn), lambda m, n: (m, n)),
        out_shape=jax.ShapeDtypeStruct((M, D), BF16),
        compiler_params=_params(_ARB2),
        name="merge",
    )(att, cn, wa, wc, proj, proj)


def _final_norm_kernel(h_ref, g_ref, o_ref):
    x = h_ref[...]
    ms = jnp.mean(x * x, axis=-1, keepdims=True)
    o_ref[...] = x * lax.rsqrt(ms + EPS) * g_ref[...]


def _final_norm(h, g, B, L, n_meta, tr):
    D = h.shape[1]
    S = L - n_meta
    assert L % 8 == 0 and n_meta % 8 == 0 and tr % 8 == 0
    return pl.pallas_call(
        _final_norm_kernel,
        grid=(B, S // tr),
        in_specs=[pl.BlockSpec((pl.Element(tr), pl.Element(D)),
                               lambda b, i: (pl.multiple_of(b * L + n_meta + i * tr, 8), 0)),
                  pl.BlockSpec((1, D), lambda b, i: (0, 0))],
        out_specs=pl.BlockSpec((None, tr, D), lambda b, i: (b, i, 0)),
        out_shape=jax.ShapeDtypeStruct((B, S, D), F32),
        compiler_params=_params(_ARB2),
        name="final_norm",
    )(h, g)


def _conv_tile(L):
    for mult in (48, 32, V7X_BF16_SUBLANES):
        if L % mult == 0:
            return _divisor_tile(L, 2100, mult)
    raise ValueError(f"sequence length {L} is not a multiple of {V7X_BF16_SUBLANES}")


def _tiles(M, L, S, D, F, q_width, gate_start, d_in):
    sub = V7X_BF16_SUBLANES
    return dict(
        tm_norm=_divisor_tile(M, 704, sub),
        tm=_divisor_tile(M, 1024, sub),
        tf=_divisor_tile(F, 512, V7X_LANES),
        tn_res=_divisor_tile(D, 1024, V7X_LANES),
        tm_down=_divisor_tile(M, 2100, sub),
        tn_down=_divisor_tile(D, 512, V7X_LANES),
        tn_proj=_divisor_tile(math.gcd(q_width, gate_start, d_in), 1024, V7X_LANES),
        tn_merge=_divisor_tile(math.gcd(D, gate_start), 1024, V7X_LANES),
        tl=_conv_tile(L),
        tr=_divisor_tile(S, 512, 8),
        tq_target=704 if L <= 4096 else 448,
        tk=256,
    )


def _encode(x, meta, rel_bias, wts, final_norm, dims):
    H, dh, V = dims["H"], dims["dh"], dims["V"]
    B, S, D = x.shape
    n_meta = meta.shape[0]
    L = S + n_meta
    M = B * L
    depth = wts["w_in"].shape[0]
    F = wts["w_ffn1_gate"].shape[-1]
    KW, C = wts["conv_w"].shape[1:]
    d_in = wts["w_in"].shape[-1]
    qk_width = H * 2 * dh
    u_start = 2 * qk_width + H * V
    gate_start = u_start + 2 * C
    assert d_in == gate_start + 2 * D

    t = _tiles(M, L, S, D, F, qk_width, gate_start, d_in)
    plan = _AttnPlan(L, t["tq_target"], t["tk"])
    bank = _bias_bank(rel_bias, plan)
    qscale = dh ** -0.5 * LOG2E

    h = jnp.concatenate([jnp.broadcast_to(meta[None].astype(x.dtype), (B, n_meta, D)), x],
                        axis=1).reshape(M, D)
    ssq = _row_ssq(h, t["tm_norm"])
    row = lambda v: v.reshape(1, -1)
    for l in range(depth):
        lam_init = 0.8 - 0.6 * math.exp(-0.3 * l)
        hm = _ffn_up(h, ssq, row(wts["norm_ffn1"][l]), wts["w_ffn1_gate"], wts["w_ffn1_up"], l,
                     t["tm_norm"], t["tf"])
        h, ssq = _mm_res(hm, wts["w_ffn1_down"], l, h, 0.5, t["tm_down"], t["tn_down"])

        proj = _proj(h, ssq, row(wts["norm_mix"][l]), wts["w_in"], l, t["tm_norm"], t["tn_proj"],
                     qk_width, gate_start, qscale)
        proj3 = proj.reshape(B, L, d_in)
        att = _attention(proj3, bank, rel_bias, row(wts["lambda_q1"][l]), row(wts["lambda_k1"][l]),
                         row(wts["lambda_q2"][l]), row(wts["lambda_k2"][l]), row(wts["subln_g"][l]),
                         plan, H, dh, V, lam_init)
        cn = _conv_module(proj3, wts["conv_w"][l], row(wts["conv_b"][l]), row(wts["conv_ln_g"][l]),
                          row(wts["conv_ln_b"][l]), u_start, t["tl"])
        mg = _merge(att.reshape(M, H * V), cn.reshape(M, C), wts["w_attn_out"],
                    wts["w_conv_out"], l, proj, gate_start, t["tm"], t["tn_merge"])
        h, ssq = _mm_res(mg, wts["w_out"], l, h, 1.0, t["tm"], t["tn_res"])

        hm = _ffn_up(h, ssq, row(wts["norm_ffn2"][l]), wts["w_ffn2_gate"], wts["w_ffn2_up"], l,
                     t["tm_norm"], t["tf"])
        h, ssq = _mm_res(hm, wts["w_ffn2_down"], l, h, 0.5, t["tm_down"], t["tn_down"])
    return _final_norm(h, row(final_norm), B, L, n_meta, t["tr"])


_MATMUL_WEIGHTS = ("w_ffn1_gate", "w_ffn1_up", "w_ffn1_down", "w_in", "w_attn_out",
                   "w_conv_out", "w_out", "w_ffn2_gate", "w_ffn2_up", "w_ffn2_down")


def kernel(x_prompt, x_sample, meta_tokens, rel_bias, norm_ffn1, w_ffn1_gate, w_ffn1_up, w_ffn1_down, norm_mix, w_in, lambda_q1, lambda_k1, lambda_q2, lambda_k2, subln_g, w_attn_out, conv_w, conv_b, conv_ln_g, conv_ln_b, w_conv_out, w_out, norm_ffn2, w_ffn2_gate, w_ffn2_up, w_ffn2_down, final_norm):
    wts = dict(norm_ffn1=norm_ffn1, w_ffn1_gate=w_ffn1_gate, w_ffn1_up=w_ffn1_up,
               w_ffn1_down=w_ffn1_down, norm_mix=norm_mix, w_in=w_in, lambda_q1=lambda_q1,
               lambda_k1=lambda_k1, lambda_q2=lambda_q2, lambda_k2=lambda_k2, subln_g=subln_g,
               w_attn_out=w_attn_out, conv_w=conv_w, conv_b=conv_b, conv_ln_g=conv_ln_g,
               conv_ln_b=conv_ln_b, w_conv_out=w_conv_out, w_out=w_out, norm_ffn2=norm_ffn2,
               w_ffn2_gate=w_ffn2_gate, w_ffn2_up=w_ffn2_up, w_ffn2_down=w_ffn2_down)
    for name in _MATMUL_WEIGHTS:
        wts[name] = wts[name].astype(BF16)
    dh = lambda_q1.shape[-1]
    V = subln_g.shape[-1]
    dims = dict(H=rel_bias.shape[1], dh=dh, V=V)
    y_prompt = _encode(x_prompt, meta_tokens, rel_bias, wts, final_norm, dims)
    y_sample = _encode(x_sample, meta_tokens, rel_bias, wts, final_norm, dims)
    return (y_prompt, y_sample)
```
